```python
import numpy as np
import jax
import jax.numpy as jnp
from jax import lax

D_MODEL = 1024
BATCH = 2
SEQ = 8192
DEPTH = 4
DEC_BATCH = 128
DEC_SEQ = 8
PAST_LEN = 2048
PAGE_SIZE = 128

N_MIXERS = 3
HEAD_DIM = 64
MIX_W = 3 * D_MODEL // 4
MEM_W = D_MODEL - MIX_W
MEM_HEADS = MEM_W // HEAD_DIM
N_MEM = 256
NSA_HEADS = MIX_W // HEAD_DIM
NSA_KV_HEADS = 4
NSA_GROUP = NSA_HEADS // NSA_KV_HEADS
NSA_KV_W = 2 * NSA_KV_HEADS * HEAD_DIM
CMP_STRIDE = 16
CMP_LEN = 2 * CMP_STRIDE
CMP_HIDDEN = 64
SEL_BLOCK = 64
N_SEL = 16
WINDOW = 512
NSA_QBLOCK = 128
FORCE_BONUS = 1.0e4
A_IN_W = MIX_W + 3 * NSA_KV_W + 3 * NSA_HEADS + MEM_W
CONV_B_WIDTH = 3
B_IN_W = 3 * MIX_W + MEM_W
CONV_C_WIDTH = 31
C_IN_W = 2 * MIX_W + MEM_W
PEER_HEADS = 8
PEER_KEYS = 128
N_EXPERTS = PEER_KEYS * PEER_KEYS
PEER_TOPK = 16
PEER_DK = 256
PEER_CHUNK = 128
N_LAYERS_A = (DEPTH + N_MIXERS - 1) // N_MIXERS
N_LAYERS_B = (DEPTH + N_MIXERS - 2) // N_MIXERS
N_LAYERS_C = DEPTH // N_MIXERS
EPS = 1e-6
NEG = -1e30

kernel_name = "hybrid_nsa_shortconv_conformer_peer_step"


def rms_norm(x, g):
    xf = x.astype(jnp.float32)
    y = xf * lax.rsqrt(jnp.mean(xf * xf, axis=-1, keepdims=True) + EPS)
    return (y * g.astype(jnp.float32)).astype(x.dtype)


def layer_norm(x, g, b):
    xf = x.astype(jnp.float32)
    mu = jnp.mean(xf, axis=-1, keepdims=True)
    var = jnp.mean(jnp.square(xf - mu), axis=-1, keepdims=True)
    y = (xf - mu) * lax.rsqrt(var + EPS)
    return (y * g.astype(jnp.float32) + b.astype(jnp.float32)).astype(x.dtype)


def masked_softmax(s, mask):
    p = jax.nn.softmax(jnp.where(mask, s.astype(jnp.float32), NEG), axis=-1)
    return jnp.where(mask, p, 0.0)


def causal_dwconv(u_ext, w, b):
    y = lax.conv_general_dilated(u_ext, w[:, None, :].astype(u_ext.dtype), window_strides=(1,),
                                 padding="VALID", dimension_numbers=("NWC", "WIO", "NWC"),
                                 feature_group_count=u_ext.shape[-1])
    return y + b


def memory_kv(mem, g_norm, w_kv, k_g):
    n, m, _ = mem.shape
    kv = (rms_norm(mem, g_norm) @ w_kv).reshape(n, m, 2, MEM_HEADS, HEAD_DIM)
    return jnp.stack([rms_norm(kv[:, :, 0], k_g), kv[:, :, 1]], axis=2)


def memory_attend(zq, mkv, q_g):
    n, t, _ = zq.shape
    q = rms_norm(zq.reshape(n, t, MEM_HEADS, HEAD_DIM), q_g)
    s = jnp.einsum("nthd,nmhd->nhtm", q, mkv[:, :, 0]) * (HEAD_DIM ** -0.5)
    p = jax.nn.softmax(s.astype(jnp.float32), axis=-1).astype(zq.dtype)
    return jnp.einsum("nhtm,nmhd->nthd", p, mkv[:, :, 1]).reshape(n, t, MEM_W)


def nsa_split(z, q_g, k_g):
    n, t, _ = z.shape
    G, R, HD = NSA_KV_HEADS, NSA_GROUP, HEAD_DIM
    q = rms_norm(z[..., :MIX_W].reshape(n, t, G, R, HD), q_g)
    o = MIX_W
    cmp_kv = z[..., o:o + NSA_KV_W].reshape(n, t, 2, G, HD)
    sel = z[..., o + NSA_KV_W:o + 2 * NSA_KV_W].reshape(n, t, 2, G, HD)
    win = z[..., o + 2 * NSA_KV_W:o + 3 * NSA_KV_W].reshape(n, t, 2, G, HD)
    sel_kv = jnp.stack([rms_norm(sel[:, :, 0], k_g[1]), sel[:, :, 1]], axis=2)
    win_kv = jnp.stack([rms_norm(win[:, :, 0], k_g[2]), win[:, :, 1]], axis=2)
    o = o + 3 * NSA_KV_W
    gates = jax.nn.sigmoid(z[..., o:o + 3 * NSA_HEADS]).reshape(n, t, G, R, 3)
    return q, cmp_kv, sel_kv, win_kv, gates


def compress(rows, pe, w1, b1, w2):
    n, length, G, HD = rows.shape
    n_ch = length // CMP_STRIDE
    ch = rows[:, :n_ch * CMP_STRIDE].reshape(n, n_ch, CMP_STRIDE, G, HD)
    a = jnp.einsum("ncsgd,ksdh->nckgh", ch, w1.reshape(2, CMP_STRIDE, HD, CMP_HIDDEN))
    bias = pe.reshape(-1) @ w1 + b1
    pre = a[:, :-1, 0] + a[:, 1:, 1] + bias
    return jax.nn.gelu(pre, approximate=False) @ w2


def selection_map(n_cmp, n_sel):
    start = np.arange(n_cmp)[:, None] * CMP_STRIDE
    blk = np.arange(n_sel)[None, :] * SEL_BLOCK
    m = (start <= blk + SEL_BLOCK - 1) & (start + CMP_LEN - 1 >= blk)
    return jnp.asarray(m.astype(np.float32))


def nsa_block(q, q_pos, kc, vc, c_end, ks, vs, kw, vw, w_pos, gates, sel_map):
    scale = HEAD_DIM ** -0.5
    tq = q.shape[0]
    n_sel = sel_map.shape[1]
    m_c = (c_end[None, :] <= q_pos[:, None])[:, None, None, :]
    p_c = masked_softmax(jnp.einsum("tgrd,ngd->tgrn", q, kc) * scale, m_c)
    o_c = jnp.einsum("tgrn,ngd->tgrd", p_c.astype(vc.dtype), vc)
    imp = jnp.einsum("tgn,ns->tgs", jnp.sum(p_c, axis=2), sel_map)
    blk = jnp.arange(n_sel)[None, :]
    cur = (q_pos // SEL_BLOCK)[:, None]
    forced = (blk == 0) | (blk == cur) | (blk == cur - 1)
    valid = blk * SEL_BLOCK <= q_pos[:, None]
    bonus = jnp.where(forced, FORCE_BONUS, 0.0)
    imp = jnp.where(valid[:, None, :], imp + bonus[:, None, :], NEG)
    _, idx = lax.top_k(imp, min(N_SEL, n_sel))
    k_eff = idx.shape[-1]
    g_idx = jnp.arange(NSA_KV_HEADS)[None, :, None]

    def gather_blocks(a):
        a = a.reshape(n_sel, SEL_BLOCK, NSA_KV_HEADS, HEAD_DIM).transpose(2, 0, 1, 3)
        return a[g_idx, idx].reshape(tq, NSA_KV_HEADS, k_eff * SEL_BLOCK, HEAD_DIM)

    k_s = gather_blocks(ks)
    v_s = gather_blocks(vs)
    pos_s = (idx[..., None] * SEL_BLOCK + jnp.arange(SEL_BLOCK)).reshape(tq, NSA_KV_HEADS, k_eff * SEL_BLOCK)
    m_s = (pos_s <= q_pos[:, None, None])[:, :, None, :]
    p_s = masked_softmax(jnp.einsum("tgrd,tgld->tgrl", q, k_s) * scale, m_s)
    o_s = jnp.einsum("tgrl,tgld->tgrd", p_s.astype(v_s.dtype), v_s)
    dist = q_pos[:, None] - w_pos[None, :]
    m_w = ((dist >= 0) & (dist <= WINDOW) & (w_pos[None, :] >= 0))[:, None, None, :]
    p_w = masked_softmax(jnp.einsum("tgrd,lgd->tgrl", q, kw) * scale, m_w)
    o_w = jnp.einsum("tgrl,lgd->tgrd", p_w.astype(vw.dtype), vw)
    return gates[..., 0:1] * o_c + gates[..., 1:2] * o_s + gates[..., 2:3] * o_w


def nsa_prompt(z, q_g, k_g, pe, w1, b1, w2):
    n, s, _ = z.shape
    q, cmp_kv, sel_kv, win_kv, gates = nsa_split(z, q_g, k_g)
    kc = rms_norm(compress(cmp_kv[:, :, 0], pe[0], w1[0], b1[0], w2[0]), k_g[0])
    vc = compress(cmp_kv[:, :, 1], pe[1], w1[1], b1[1], w2[1])
    n_cmp = kc.shape[1]
    c_end = jnp.arange(n_cmp) * CMP_STRIDE + (CMP_LEN - 1)
    sel_map = selection_map(n_cmp, s // SEL_BLOCK)
    pad = ((0, 0), (WINDOW, 0), (0, 0), (0, 0))
    kw = jnp.pad(win_kv[:, :, 0], pad)
    vw = jnp.pad(win_kv[:, :, 1], pad)
    nb = s // NSA_QBLOCK
    span = WINDOW + NSA_QBLOCK

    def step(i):
        b = i // nb
        start = (i % nb) * NSA_QBLOCK
        q_pos = start + jnp.arange(NSA_QBLOCK)
        w_pos = start - WINDOW + jnp.arange(span)
        return nsa_block(lax.dynamic_slice_in_dim(q[b], start, NSA_QBLOCK, 0), q_pos, kc[b], vc[b], c_end,
                         sel_kv[b, :, 0], sel_kv[b, :, 1],
                         lax.dynamic_slice_in_dim(kw[b], start, span, 0),
                         lax.dynamic_slice_in_dim(vw[b], start, span, 0), w_pos,
                         lax.dynamic_slice_in_dim(gates[b], start, NSA_QBLOCK, 0), sel_map)

    out = lax.map(step, jnp.arange(n * nb)).reshape(n, s, MIX_W)
    pages = lambda a: a.reshape(n, s // PAGE_SIZE, PAGE_SIZE, 2, NSA_KV_HEADS, HEAD_DIM)
    wb = min(WINDOW, s)
    return out, pages(cmp_kv), pages(sel_kv), win_kv[:, s - wb:]


def nsa_sample(z, pool_cmp, pool_sel, win_buf, page_table, q_g, k_g, pe, w1, b1, w2):
    n, t_new, _ = z.shape
    past = page_table.shape[1] * PAGE_SIZE
    q, cmp_new, sel_new, win_new, gates = nsa_split(z, q_g, k_g)
    gather = lambda pool: pool[page_table].reshape(n, past, 2, NSA_KV_HEADS, HEAD_DIM)
    cmp_all = jnp.concatenate([gather(pool_cmp), cmp_new], axis=1)
    total = past + t_new
    n_sel = -(-total // SEL_BLOCK)
    sel_all = jnp.pad(jnp.concatenate([gather(pool_sel), sel_new], axis=1),
                      ((0, 0), (0, n_sel * SEL_BLOCK - total), (0, 0), (0, 0), (0, 0)))
    win_all = jnp.concatenate([win_buf, win_new], axis=1)
    wb = win_buf.shape[1]
    kc = rms_norm(compress(cmp_all[:, :, 0], pe[0], w1[0], b1[0], w2[0]), k_g[0])
    vc = compress(cmp_all[:, :, 1], pe[1], w1[1], b1[1], w2[1])
    n_cmp = kc.shape[1]
    c_end = jnp.arange(n_cmp) * CMP_STRIDE + (CMP_LEN - 1)
    sel_map = selection_map(n_cmp, n_sel)
    q_pos = past + jnp.arange(t_new)
    w_pos = past - wb + jnp.arange(wb + t_new)

    def step(a):
        qq, kcc, vcc, kss, vss, kww, vww, gg = a
        return nsa_block(qq, q_pos, kcc, vcc, c_end, kss, vss, kww, vww, w_pos, gg, sel_map)

    out = lax.map(step, (q, kc, vc, sel_all[:, :, 0], sel_all[:, :, 1], win_all[:, :, 0], win_all[:, :, 1], gates))
    return out.reshape(n, t_new, MIX_W), cmp_new, sel_new, win_all[:, -wb:]


def shortconv_mix(z, ctx, w, b):
    bg = z[..., :MIX_W]
    cg = z[..., MIX_W:2 * MIX_W]
    v = z[..., 2 * MIX_W:3 * MIX_W]
    u_ext = jnp.concatenate([ctx, cg * v], axis=1)
    y = bg * causal_dwconv(u_ext, w, b)
    return y, u_ext[:, -(CONV_B_WIDTH - 1):]


def conformer_mix(z, ctx, w, b, ln_g, ln_b):
    u = z[..., :MIX_W] * jax.nn.sigmoid(z[..., MIX_W:2 * MIX_W])
    u_ext = jnp.concatenate([ctx, u], axis=1)
    y = jax.nn.silu(layer_norm(causal_dwconv(u_ext, w, b), ln_g, ln_b))
    return y, u_ext[:, -(CONV_C_WIDTH - 1):]


def peer_chunk(hc, w_q, sub_keys, u_tab, v_tab):
    c = hc.shape[0]
    q = (hc @ w_q).reshape(c, PEER_HEADS, 2, PEER_DK // 2)
    s = jnp.einsum("nhcd,hckd->nhck", q, sub_keys).astype(jnp.float32)
    s1, i1 = lax.top_k(s[:, :, 0], PEER_TOPK)
    s2, i2 = lax.top_k(s[:, :, 1], PEER_TOPK)
    cand = (s1[..., :, None] + s2[..., None, :]).reshape(c, PEER_HEADS, PEER_TOPK * PEER_TOPK)
    cid = (i1[..., :, None] * PEER_KEYS + i2[..., None, :]).reshape(c, PEER_HEADS, PEER_TOPK * PEER_TOPK)
    top, j = lax.top_k(cand, PEER_TOPK)
    e = jnp.take_along_axis(cid, j, axis=-1)
    g = jax.nn.softmax(top, axis=-1).astype(hc.dtype)
    act = jax.nn.gelu(jnp.einsum("nd,nhkd->nhk", hc, u_tab[e]), approximate=False)
    return jnp.einsum("nhk,nhkd->nd", g * act, v_tab[e])


def peer_ffn(h, w_q, sub_keys, u_tab, v_tab):
    lead = h.shape[:-1]
    hf = h.reshape(-1, D_MODEL)
    n = hf.shape[0]
    n_pad = -(-n // PEER_CHUNK) * PEER_CHUNK
    hf = jnp.pad(hf, ((0, n_pad - n), (0, 0))).reshape(-1, PEER_CHUNK, D_MODEL)
    out = lax.map(lambda hc: peer_chunk(hc, w_q, sub_keys, u_tab, v_tab), hf)
    return out.reshape(n_pad, D_MODEL)[:n].reshape(*lead, D_MODEL)


def setup_inputs(seed: int = 0) -> dict:
    key = jax.random.key(seed)
    keys = iter(jax.random.split(key, 48))

    def rnd(shape, scale):
        return jax.random.normal(next(keys), shape, jnp.float32) * scale

    def gain(shape):
        return 1.0 + rnd(shape, 0.05)

    n_pages = PAST_LEN // PAGE_SIZE
    used = DEC_BATCH * n_pages
    n_pool = used + (used + 3) // 4
    wb = min(WINDOW, PAST_LEN)
    G, HD, D = NSA_KV_HEADS, HEAD_DIM, D_MODEL
    inputs = {
        "x_prompt": rnd((BATCH, SEQ, D), 1.0),
        "x_sample": rnd((DEC_BATCH, DEC_SEQ, D), 1.0),
        "cache_a_cmp_kv": rnd((N_LAYERS_A, n_pool, PAGE_SIZE, 2, G, HD), 1.0),
        "cache_a_sel_kv": rnd((N_LAYERS_A, n_pool, PAGE_SIZE, 2, G, HD), 1.0),
        "cache_a_win_kv": rnd((N_LAYERS_A, DEC_BATCH, wb, 2, G, HD), 1.0),
        "state_b_conv": rnd((N_LAYERS_B, DEC_BATCH, CONV_B_WIDTH - 1, MIX_W), 1.0),
        "state_c_conv": rnd((N_LAYERS_C, DEC_BATCH, CONV_C_WIDTH - 1, MIX_W), 1.0),
        "cache_mem_kv": rnd((DEPTH, DEC_BATCH, N_MEM, 2, MEM_HEADS, HD), 1.0),
        "page_table": jax.random.permutation(next(keys), n_pool)[:used].reshape(DEC_BATCH, n_pages).astype(jnp.int32),
        "mem_prompt": rnd((BATCH, N_MEM, D), 1.0),
        "norm_mix_g": gain((DEPTH, D)),
        "norm_mem_g": gain((DEPTH, D)),
        "w_mem_kv": rnd((DEPTH, D, 2 * MEM_W), D ** -0.5),
        "mem_q_norm_g": gain((DEPTH, HD)),
        "mem_k_norm_g": gain((DEPTH, HD)),
        "w_out": rnd((DEPTH, D, D), D ** -0.5),
        "norm_ffn_g": gain((DEPTH, D)),
        "peer_w_q": rnd((DEPTH, D, PEER_HEADS * PEER_DK), D ** -0.5),
        "peer_sub_keys": rnd((DEPTH, PEER_HEADS, 2, PEER_KEYS, PEER_DK // 2), (PEER_DK // 2) ** -0.5),
        "peer_u": rnd((DEPTH, N_EXPERTS, D), D ** -0.5),
        "peer_v": rnd((DEPTH, N_EXPERTS, D), 0.2),
        "a_w_in": rnd((N_LAYERS_A, D, A_IN_W), D ** -0.5),
        "a_q_norm_g": gain((N_LAYERS_A, HD)),
        "a_k_norm_g": gain((N_LAYERS_A, 3, HD)),
        "a_cmp_pe": rnd((N_LAYERS_A, 2, CMP_LEN, HD), 0.5),
        "a_cmp_w1": rnd((N_LAYERS_A, 2, CMP_LEN * HD, CMP_HIDDEN), (CMP_LEN * HD) ** -0.5),
        "a_cmp_b1": rnd((N_LAYERS_A, 2, CMP_HIDDEN), 0.02),
        "a_cmp_w2": rnd((N_LAYERS_A, 2, CMP_HIDDEN, HD), CMP_HIDDEN ** -0.5),
        "b_w_in": rnd((N_LAYERS_B, D, B_IN_W), D ** -0.5),
        "b_conv_w": rnd((N_LAYERS_B, CONV_B_WIDTH, MIX_W), CONV_B_WIDTH ** -0.5),
        "b_conv_b": rnd((N_LAYERS_B, MIX_W), 0.02),
        "c_w_in": rnd((N_LAYERS_C, D, C_IN_W), D ** -0.5),
        "c_conv_w": rnd((N_LAYERS_C, CONV_C_WIDTH, MIX_W), CONV_C_WIDTH ** -0.5),
        "c_conv_b": rnd((N_LAYERS_C, MIX_W), 0.02),
        "c_ln_g": gain((N_LAYERS_C, MIX_W)),
        "c_ln_b": rnd((N_LAYERS_C, MIX_W), 0.02),
    }
    return inputs


def reference(x_prompt, x_sample, cache_a_cmp_kv, cache_a_sel_kv, cache_a_win_kv, state_b_conv, state_c_conv,
              cache_mem_kv, page_table, mem_prompt, norm_mix_g, norm_mem_g, w_mem_kv, mem_q_norm_g, mem_k_norm_g,
              w_out, norm_ffn_g, peer_w_q, peer_sub_keys, peer_u, peer_v, a_w_in, a_q_norm_g, a_k_norm_g,
              a_cmp_pe, a_cmp_w1, a_cmp_b1, a_cmp_w2, b_w_in, b_conv_w, b_conv_b, c_w_in, c_conv_w, c_conv_b,
              c_ln_g, c_ln_b):
    xp, xs = x_prompt, x_sample
    n_p, n_s = xp.shape[0], xs.shape[0]
    p_cmp, p_sel, p_win, p_cb, p_cc, p_mem = [], [], [], [], [], []
    s_cmp, s_sel, s_win, s_cb, s_cc = [], [], [], [], []
    for i in range(DEPTH):
        kind, li = i % N_MIXERS, i // N_MIXERS
        hp = rms_norm(xp, norm_mix_g[i])
        hs = rms_norm(xs, norm_mix_g[i])
        if kind == 0:
            zp = hp @ a_w_in[li]
            zs = hs @ a_w_in[li]
            nsa_w = (a_q_norm_g[li], a_k_norm_g[li], a_cmp_pe[li], a_cmp_w1[li], a_cmp_b1[li], a_cmp_w2[li])
            mp, c_p, sl_p, w_p = nsa_prompt(zp, *nsa_w)
            ms, c_s, sl_s, w_s = nsa_sample(zs, cache_a_cmp_kv[li], cache_a_sel_kv[li], cache_a_win_kv[li],
                                            page_table, *nsa_w)
            p_cmp.append(c_p)
            p_sel.append(sl_p)
            p_win.append(w_p)
            s_cmp.append(c_s)
            s_sel.append(sl_s)
            s_win.append(w_s)
        elif kind == 1:
            zp = hp @ b_w_in[li]
            zs = hs @ b_w_in[li]
            mp, st_p = shortconv_mix(zp, jnp.zeros((n_p, CONV_B_WIDTH - 1, MIX_W), zp.dtype), b_conv_w[li], b_conv_b[li])
            ms, st_s = shortconv_mix(zs, state_b_conv[li], b_conv_w[li], b_conv_b[li])
            p_cb.append(st_p)
            s_cb.append(st_s)
        else:
            zp = hp @ c_w_in[li]
            zs = hs @ c_w_in[li]
            mp, st_p = conformer_mix(zp, jnp.zeros((n_p, CONV_C_WIDTH - 1, MIX_W), zp.dtype),
                                     c_conv_w[li], c_conv_b[li], c_ln_g[li], c_ln_b[li])
            ms, st_s = conformer_mix(zs, state_c_conv[li], c_conv_w[li], c_conv_b[li], c_ln_g[li], c_ln_b[li])
            p_cc.append(st_p)
            s_cc.append(st_s)
        mkv_p = memory_kv(mem_prompt, norm_mem_g[i], w_mem_kv[i], mem_k_norm_g[i])
        p_mem.append(mkv_p)
        op = jnp.concatenate([mp, memory_attend(zp[..., -MEM_W:], mkv_p, mem_q_norm_g[i])], axis=-1)
        os_ = jnp.concatenate([ms, memory_attend(zs[..., -MEM_W:], cache_mem_kv[i], mem_q_norm_g[i])], axis=-1)
        xp = xp + op @ w_out[i]
        xs = xs + os_ @ w_out[i]
        xp = xp + peer_ffn(rms_norm(xp, norm_ffn_g[i]), peer_w_q[i], peer_sub_keys[i], peer_u[i], peer_v[i])
        xs = xs + peer_ffn(rms_norm(xs, norm_ffn_g[i]), peer_w_q[i], peer_sub_keys[i], peer_u[i], peer_v[i])
    return (xp, xs, jnp.stack(p_cmp), jnp.stack(p_sel), jnp.stack(p_win), jnp.stack(p_cb), jnp.stack(p_cc),
            jnp.stack(p_mem), jnp.stack(s_cmp), jnp.stack(s_sel), jnp.stack(s_win), jnp.stack(s_cb), jnp.stack(s_cc))
```

```python
import functools

import numpy as np
import jax
import jax.numpy as jnp
from jax import lax
from jax.experimental import pallas as pl
from jax.experimental.pallas import tpu as pltpu

D_MODEL = 1024
BATCH = 2
SEQ = 8192
DEPTH = 4
DEC_BATCH = 128
DEC_SEQ = 8
PAST_LEN = 2048
PAGE_SIZE = 128

N_MIXERS = 3
HEAD_DIM = 64
MIX_W = 3 * D_MODEL // 4
MEM_W = D_MODEL - MIX_W
MEM_HEADS = MEM_W // HEAD_DIM
N_MEM = 256
NSA_HEADS = MIX_W // HEAD_DIM
NSA_KV_HEADS = 4
NSA_GROUP = NSA_HEADS // NSA_KV_HEADS
NSA_KV_W = 2 * NSA_KV_HEADS * HEAD_DIM
CMP_STRIDE = 16
CMP_LEN = 2 * CMP_STRIDE
CMP_HIDDEN = 64
SEL_BLOCK = 64
N_SEL = 16
WINDOW = 512
NSA_QBLOCK = 128
FORCE_BONUS = 1.0e4
A_IN_W = MIX_W + 3 * NSA_KV_W + 3 * NSA_HEADS + MEM_W
CONV_B_WIDTH = 3
B_IN_W = 3 * MIX_W + MEM_W
CONV_C_WIDTH = 31
C_IN_W = 2 * MIX_W + MEM_W
PEER_HEADS = 8
PEER_KEYS = 128
N_EXPERTS = PEER_KEYS * PEER_KEYS
PEER_TOPK = 16
PEER_DK = 256
EPS = 1e-6
NEG = -1e30

LANES = 128
VMEM_BYTES_V7X = 64 * 1024 * 1024

_F32 = jnp.float32
_BF16 = jnp.bfloat16
_HI = lax.Precision.HIGHEST


def rms_norm(x, g):
    xf = x.astype(jnp.float32)
    y = xf * lax.rsqrt(jnp.mean(xf * xf, axis=-1, keepdims=True) + EPS)
    return (y * g.astype(jnp.float32)).astype(x.dtype)


def layer_norm(x, g, b):
    xf = x.astype(jnp.float32)
    mu = jnp.mean(xf, axis=-1, keepdims=True)
    var = jnp.mean(jnp.square(xf - mu), axis=-1, keepdims=True)
    y = (xf - mu) * lax.rsqrt(var + EPS)
    return (y * g.astype(jnp.float32) + b.astype(jnp.float32)).astype(x.dtype)


def masked_softmax(s, mask):
    p = jax.nn.softmax(jnp.where(mask, s.astype(jnp.float32), NEG), axis=-1)
    return jnp.where(mask, p, 0.0)


def causal_dwconv(u_ext, w, b):
    y = lax.conv_general_dilated(u_ext, w[:, None, :].astype(u_ext.dtype), window_strides=(1,),
                                 padding="VALID", dimension_numbers=("NWC", "WIO", "NWC"),
                                 feature_group_count=u_ext.shape[-1])
    return y + b


def memory_kv(mem, g_norm, w_kv, k_g):
    n, m, _ = mem.shape
    kv = (rms_norm(mem, g_norm) @ w_kv).reshape(n, m, 2, MEM_HEADS, HEAD_DIM)
    return jnp.stack([rms_norm(kv[:, :, 0], k_g), kv[:, :, 1]], axis=2)


def memory_attend(zq, mkv, q_g):
    n, t, _ = zq.shape
    q = rms_norm(zq.reshape(n, t, MEM_HEADS, HEAD_DIM), q_g)
    s = jnp.einsum("nthd,nmhd->nhtm", q, mkv[:, :, 0]) * (HEAD_DIM ** -0.5)
    p = jax.nn.softmax(s.astype(jnp.float32), axis=-1).astype(zq.dtype)
    return jnp.einsum("nhtm,nmhd->nthd", p, mkv[:, :, 1]).reshape(n, t, MEM_W)


def nsa_split(z, q_g, k_g):
    n, t, _ = z.shape
    G, R, HD = NSA_KV_HEADS, NSA_GROUP, HEAD_DIM
    q = rms_norm(z[..., :MIX_W].reshape(n, t, G, R, HD), q_g)
    o = MIX_W
    cmp_kv = z[..., o:o + NSA_KV_W].reshape(n, t, 2, G, HD)
    sel = z[..., o + NSA_KV_W:o + 2 * NSA_KV_W].reshape(n, t, 2, G, HD)
    win = z[..., o + 2 * NSA_KV_W:o + 3 * NSA_KV_W].reshape(n, t, 2, G, HD)
    sel_kv = jnp.stack([rms_norm(sel[:, :, 0], k_g[1]), sel[:, :, 1]], axis=2)
    win_kv = jnp.stack([rms_norm(win[:, :, 0], k_g[2]), win[:, :, 1]], axis=2)
    o = o + 3 * NSA_KV_W
    gates = jax.nn.sigmoid(z[..., o:o + 3 * NSA_HEADS]).reshape(n, t, G, R, 3)
    return q, cmp_kv, sel_kv, win_kv, gates


def compress(rows, pe, w1, b1, w2):
    n, length, G, HD = rows.shape
    n_ch = length // CMP_STRIDE
    ch = rows[:, :n_ch * CMP_STRIDE].reshape(n, n_ch, CMP_STRIDE, G, HD)
    a = jnp.einsum("ncsgd,ksdh->nckgh", ch, w1.reshape(2, CMP_STRIDE, HD, CMP_HIDDEN))
    bias = pe.reshape(-1) @ w1 + b1
    pre = a[:, :-1, 0] + a[:, 1:, 1] + bias
    return jax.nn.gelu(pre, approximate=False) @ w2


def selection_map(n_cmp, n_sel):
    start = np.arange(n_cmp)[:, None] * CMP_STRIDE
    blk = np.arange(n_sel)[None, :] * SEL_BLOCK
    m = (start <= blk + SEL_BLOCK - 1) & (start + CMP_LEN - 1 >= blk)
    return jnp.asarray(m.astype(np.float32))


def nsa_block(q, q_pos, kc, vc, c_end, ks, vs, kw, vw, w_pos, gates, sel_map):
    scale = HEAD_DIM ** -0.5
    tq = q.shape[0]
    n_sel = sel_map.shape[1]
    m_c = (c_end[None, :] <= q_pos[:, None])[:, None, None, :]
    p_c = masked_softmax(jnp.einsum("tgrd,ngd->tgrn", q, kc) * scale, m_c)
    o_c = jnp.einsum("tgrn,ngd->tgrd", p_c.astype(vc.dtype), vc)
    imp = jnp.einsum("tgn,ns->tgs", jnp.sum(p_c, axis=2), sel_map)
    blk = jnp.arange(n_sel)[None, :]
    cur = (q_pos // SEL_BLOCK)[:, None]
    forced = (blk == 0) | (blk == cur) | (blk == cur - 1)
    valid = blk * SEL_BLOCK <= q_pos[:, None]
    bonus = jnp.where(forced, FORCE_BONUS, 0.0)
    imp = jnp.where(valid[:, None, :], imp + bonus[:, None, :], NEG)
    _, idx = lax.top_k(imp, min(N_SEL, n_sel))
    k_eff = idx.shape[-1]
    g_idx = jnp.arange(NSA_KV_HEADS)[None, :, None]

    def gather_blocks(a):
        a = a.reshape(n_sel, SEL_BLOCK, NSA_KV_HEADS, HEAD_DIM).transpose(2, 0, 1, 3)
        return a[g_idx, idx].reshape(tq, NSA_KV_HEADS, k_eff * SEL_BLOCK, HEAD_DIM)

    k_s = gather_blocks(ks)
    v_s = gather_blocks(vs)
    pos_s = (idx[..., None] * SEL_BLOCK + jnp.arange(SEL_BLOCK)).reshape(tq, NSA_KV_HEADS, k_eff * SEL_BLOCK)
    m_s = (pos_s <= q_pos[:, None, None])[:, :, None, :]
    p_s = masked_softmax(jnp.einsum("tgrd,tgld->tgrl", q, k_s) * scale, m_s)
    o_s = jnp.einsum("tgrl,tgld->tgrd", p_s.astype(v_s.dtype), v_s)
    dist = q_pos[:, None] - w_pos[None, :]
    m_w = ((dist >= 0) & (dist <= WINDOW) & (w_pos[None, :] >= 0))[:, None, None, :]
    p_w = masked_softmax(jnp.einsum("tgrd,lgd->tgrl", q, kw) * scale, m_w)
    o_w = jnp.einsum("tgrl,lgd->tgrd", p_w.astype(vw.dtype), vw)
    return gates[..., 0:1] * o_c + gates[..., 1:2] * o_s + gates[..., 2:3] * o_w


def nsa_prompt(z, q_g, k_g, pe, w1, b1, w2):
    n, s, _ = z.shape
    q, cmp_kv, sel_kv, win_kv, gates = nsa_split(z, q_g, k_g)
    kc = rms_norm(compress(cmp_kv[:, :, 0], pe[0], w1[0], b1[0], w2[0]), k_g[0])
    vc = compress(cmp_kv[:, :, 1], pe[1], w1[1], b1[1], w2[1])
    n_cmp = kc.shape[1]
    c_end = jnp.arange(n_cmp) * CMP_STRIDE + (CMP_LEN - 1)
    sel_map = selection_map(n_cmp, s // SEL_BLOCK)
    pad = ((0, 0), (WINDOW, 0), (0, 0), (0, 0))
    kw = jnp.pad(win_kv[:, :, 0], pad)
    vw = jnp.pad(win_kv[:, :, 1], pad)
    nb = s // NSA_QBLOCK
    span = WINDOW + NSA_QBLOCK

    def step(i):
        b = i // nb
        start = (i % nb) * NSA_QBLOCK
        q_pos = start + jnp.arange(NSA_QBLOCK)
        w_pos = start - WINDOW + jnp.arange(span)
        return nsa_block(lax.dynamic_slice_in_dim(q[b], start, NSA_QBLOCK, 0), q_pos, kc[b], vc[b], c_end,
                         sel_kv[b, :, 0], sel_kv[b, :, 1],
                         lax.dynamic_slice_in_dim(kw[b], start, span, 0),
                         lax.dynamic_slice_in_dim(vw[b], start, span, 0), w_pos,
                         lax.dynamic_slice_in_dim(gates[b], start, NSA_QBLOCK, 0), sel_map)

    out = lax.map(step, jnp.arange(n * nb)).reshape(n, s, MIX_W)
    pages = lambda a: a.reshape(n, s // PAGE_SIZE, PAGE_SIZE, 2, NSA_KV_HEADS, HEAD_DIM)
    wb = min(WINDOW, s)
    return out, pages(cmp_kv), pages(sel_kv), win_kv[:, s - wb:]


def _copy_kernel(pt_ref, src_ref, dst_ref):
    dst_ref[0] = src_ref[...]


def _gather_pages(pool, page_table):
    n, n_pages = page_table.shape
    w = int(np.prod(pool.shape[2:]))
    flat = pool.reshape(pool.shape[0], PAGE_SIZE, w)
    out = pl.pallas_call(
        _copy_kernel,
        grid_spec=pltpu.PrefetchScalarGridSpec(
            num_scalar_prefetch=1,
            grid=(n, n_pages),
            in_specs=[pl.BlockSpec((1, PAGE_SIZE, w), lambda b, p, pt: (pt[b, p], 0, 0))],
            out_specs=pl.BlockSpec((1, 1, PAGE_SIZE, w), lambda b, p, pt: (b, p, 0, 0)),
        ),
        out_shape=jax.ShapeDtypeStruct((n, n_pages, PAGE_SIZE, w), pool.dtype),
        name="gather_pages",
    )(page_table, flat)
    return out.reshape((n, n_pages) + pool.shape[1:])


def nsa_sample(z, pool_cmp, pool_sel, win_buf, page_table, q_g, k_g, pe, w1, b1, w2):
    n, t_new, _ = z.shape
    past = page_table.shape[1] * PAGE_SIZE
    q, cmp_new, sel_new, win_new, gates = nsa_split(z, q_g, k_g)
    gather = lambda pool: _gather_pages(pool, page_table).reshape(n, past, 2, NSA_KV_HEADS, HEAD_DIM)
    cmp_all = jnp.concatenate([gather(pool_cmp), cmp_new], axis=1)
    total = past + t_new
    n_sel = -(-total // SEL_BLOCK)
    sel_all = jnp.pad(jnp.concatenate([gather(pool_sel), sel_new], axis=1),
                      ((0, 0), (0, n_sel * SEL_BLOCK - total), (0, 0), (0, 0), (0, 0)))
    win_all = jnp.concatenate([win_buf, win_new], axis=1)
    wb = win_buf.shape[1]
    kc = rms_norm(compress(cmp_all[:, :, 0], pe[0], w1[0], b1[0], w2[0]), k_g[0])
    vc = compress(cmp_all[:, :, 1], pe[1], w1[1], b1[1], w2[1])
    n_cmp = kc.shape[1]
    c_end = jnp.arange(n_cmp) * CMP_STRIDE + (CMP_LEN - 1)
    sel_map = selection_map(n_cmp, n_sel)
    q_pos = past + jnp.arange(t_new)
    w_pos = past - wb + jnp.arange(wb + t_new)

    def step(a):
        qq, kcc, vcc, kss, vss, kww, vww, gg = a
        return nsa_block(qq, q_pos, kcc, vcc, c_end, kss, vss, kww, vww, w_pos, gg, sel_map)

    out = lax.map(step, (q, kc, vc, sel_all[:, :, 0], sel_all[:, :, 1], win_all[:, :, 0], win_all[:, :, 1], gates))
    return out.reshape(n, t_new, MIX_W), cmp_new, sel_new, win_all[:, -wb:]


def shortconv_mix(z, ctx, w, b):
    bg = z[..., :MIX_W]
    cg = z[..., MIX_W:2 * MIX_W]
    v = z[..., 2 * MIX_W:3 * MIX_W]
    u_ext = jnp.concatenate([ctx, cg * v], axis=1)
    y = bg * causal_dwconv(u_ext, w, b)
    return y, u_ext[:, -(CONV_B_WIDTH - 1):]


def conformer_mix(z, ctx, w, b, ln_g, ln_b):
    u = z[..., :MIX_W] * jax.nn.sigmoid(z[..., MIX_W:2 * MIX_W])
    u_ext = jnp.concatenate([ctx, u], axis=1)
    y = jax.nn.silu(layer_norm(causal_dwconv(u_ext, w, b), ln_g, ln_b))
    return y, u_ext[:, -(CONV_C_WIDTH - 1):]


_CAND = [(j1, j2) for j1 in range(PEER_TOPK) for j2 in range(PEER_TOPK) if (j1 + 1) * (j2 + 1) <= PEER_TOPK]
_N_CAND = len(_CAND)
_CAND_ROWS = -(-_N_CAND // 8) * 8
_PEER_SEL_TILE = 256
_PEER_TOK_TILE = 512
_PEER_EXP_BLOCK = 512


def _cand_tables():
    j1 = np.full((_CAND_ROWS, LANES), -1.0, np.float32)
    j2 = np.full((_CAND_ROWS, LANES), -1.0, np.float32)
    flat = np.full((_CAND_ROWS, LANES), 1.0e6, np.float32)
    for r, (a, b) in enumerate(_CAND):
        j1[r], j2[r], flat[r] = a, b, a * PEER_TOPK + b
    return np.stack([j1, j2, flat])


def _gelu_erf(x):
    return 0.5 * x * (1.0 + lax.erf(x * (2.0 ** -0.5)))


def _top16_rows(s):
    k, l = s.shape
    kidx = lax.broadcasted_iota(jnp.int32, (k, l), 0).astype(_F32)
    jrow = lax.broadcasted_iota(jnp.int32, (PEER_TOPK, l), 0)
    work = s
    rank = jnp.full((k, l), float(PEER_TOPK), _F32)
    vals = jnp.zeros((PEER_TOPK, l), _F32)
    for j in range(PEER_TOPK):
        m = jnp.max(work, axis=0, keepdims=True)
        first = jnp.min(jnp.where(work == m, kidx, float(k)), axis=0, keepdims=True)
        sel = kidx == first
        rank = jnp.where(sel, float(j), rank)
        work = jnp.where(sel, -jnp.inf, work)
        vals = jnp.where(jrow == j, m, vals)
    return vals, rank


def _peer_select_kernel(x_ref, g_ref, wq_ref, sk_ref, ct_ref, ht_ref, p1_ref, c1_ref, p2_ref, r2_ref, h_scr):
    hd = pl.program_id(1)

    @pl.when(hd == 0)
    def _():
        x = x_ref[...]
        h = x * lax.rsqrt(jnp.mean(x * x, axis=-1, keepdims=True) + EPS) * g_ref[...]
        h_scr[...] = h
        ht_ref[...] = h.T.astype(_BF16)

    q = jnp.dot(h_scr[...], wq_ref[...], precision=_HI, preferred_element_type=_F32)
    nt = (((1,), (1,)), ((), ()))
    half = PEER_DK // 2
    s1_all = lax.dot_general(sk_ref[0, 0], q[:, :half], nt, precision=_HI, preferred_element_type=_F32)
    s2_all = lax.dot_general(sk_ref[0, 1], q[:, half:], nt, precision=_HI, preferred_element_type=_F32)
    cj1, cj2, cflat = ct_ref[0], ct_ref[1], ct_ref[2]
    for c in range(x_ref.shape[0] // LANES):
        sl = slice(c * LANES, (c + 1) * LANES)
        s1, s2 = s1_all[:, sl], s2_all[:, sl]
        v1, rank1 = _top16_rows(s1)
        v2, rank2 = _top16_rows(s2)
        a1 = jnp.zeros((_CAND_ROWS, LANES), _F32)
        a2 = jnp.zeros((_CAND_ROWS, LANES), _F32)
        for j in range(PEER_TOPK):
            a1 = jnp.where(cj1 == float(j), v1[j:j + 1, :], a1)
            a2 = jnp.where(cj2 == float(j), v2[j:j + 1, :], a2)
        cand = jnp.where(cj1 >= 0.0, a1 + a2, -jnp.inf)
        work = cand
        picked = jnp.zeros((_CAND_ROWS, LANES), _F32)
        for _ in range(PEER_TOPK):
            m = jnp.max(work, axis=0, keepdims=True)
            first = jnp.min(jnp.where(work == m, cflat, 2.0e6), axis=0, keepdims=True)
            sel = cflat == first
            picked = jnp.where(sel, 1.0, picked)
            work = jnp.where(sel, -jnp.inf, work)
        top = v1[0:1, :] + v2[0:1, :]
        z = jnp.sum(jnp.where(picked > 0.0, jnp.exp(cand - top), 0.0), axis=0, keepdims=True)
        c1 = jnp.zeros((PEER_KEYS, LANES), _F32)
        for j in range(PEER_TOPK):
            cnt_j = jnp.sum(jnp.where(cj1 == float(j), picked, 0.0), axis=0, keepdims=True)
            c1 = jnp.where(rank1 == float(j), cnt_j, c1)
        p1_ref[0, :, sl] = jnp.exp(s1 - v1[0:1, :]) / z
        c1_ref[0, :, sl] = c1
        p2_ref[0, :, sl] = jnp.exp(s2 - v2[0:1, :])
        r2_ref[0, :, sl] = rank2


def _peer_select(x, g, w_q, sub_keys):
    n, d = x.shape
    ts = _PEER_SEL_TILE
    assert n % ts == 0
    fac = jax.ShapeDtypeStruct((PEER_HEADS, PEER_KEYS, n), _F32)
    fac_spec = pl.BlockSpec((1, PEER_KEYS, ts), lambda i, h: (h, 0, i))
    return pl.pallas_call(
        _peer_select_kernel,
        grid=(n // ts, PEER_HEADS),
        in_specs=[
            pl.BlockSpec((ts, d), lambda i, h: (i, 0)),
            pl.BlockSpec((1, d), lambda i, h: (0, 0)),
            pl.BlockSpec((d, PEER_DK), lambda i, h: (0, h)),
            pl.BlockSpec((1, 2, PEER_KEYS, PEER_DK // 2), lambda i, h: (h, 0, 0, 0)),
            pl.BlockSpec((3, _CAND_ROWS, LANES), lambda i, h: (0, 0, 0)),
        ],
        out_specs=[pl.BlockSpec((d, ts), lambda i, h: (0, i)), fac_spec, fac_spec, fac_spec, fac_spec],
        out_shape=[jax.ShapeDtypeStruct((d, n), _BF16), fac, fac, fac, fac],
        scratch_shapes=[pltpu.VMEM((ts, d), _F32)],
        compiler_params=pltpu.CompilerParams(dimension_semantics=("arbitrary", "arbitrary"),
                                             vmem_limit_bytes=VMEM_BYTES_V7X * 3 // 4),
        name="peer_select",
    )(x, g.reshape(1, d), w_q, sub_keys, jnp.asarray(_cand_tables()))


def _peer_dense_kernel(x_ref, ht_ref, u_ref, vt_ref, p1_ref, c1_ref, p2_ref, r2_ref, o_ref, acc_ref):
    j = pl.program_id(1)

    @pl.when(j == 0)
    def _():
        acc_ref[...] = jnp.zeros_like(acc_ref)

    eb = u_ref.shape[0]
    st = jnp.dot(u_ref[...], ht_ref[...], preferred_element_type=_F32)
    act = _gelu_erf(st)
    wact = []
    for al in range(eb // PEER_KEYS):
        a = j * (eb // PEER_KEYS) + al
        w = jnp.zeros((PEER_KEYS, st.shape[1]), _F32)
        for h in range(PEER_HEADS):
            c1 = c1_ref[h, pl.ds(a, 1), :]
            p1 = p1_ref[h, pl.ds(a, 1), :]
            w = w + jnp.where(r2_ref[h] < c1, p2_ref[h], 0.0) * p1
        wact.append((w * act[al * PEER_KEYS:(al + 1) * PEER_KEYS, :]).astype(_BF16))
    wact = jnp.concatenate(wact, axis=0)
    acc_ref[...] += jnp.dot(vt_ref[...], wact, preferred_element_type=_F32)

    @pl.when(j == pl.num_programs(1) - 1)
    def _():
        o_ref[...] = x_ref[...] + acc_ref[...].T


def _peer_dense(x, ht, u_bf, vt_bf, p1, c1, p2, r2):
    n, d = x.shape
    tt, eb = _PEER_TOK_TILE, _PEER_EXP_BLOCK
    assert n % tt == 0 and N_EXPERTS % eb == 0
    fac_spec = pl.BlockSpec((PEER_HEADS, PEER_KEYS, tt), lambda i, j: (0, 0, i))
    return pl.pallas_call(
        _peer_dense_kernel,
        grid=(n // tt, N_EXPERTS // eb),
        in_specs=[
            pl.BlockSpec((tt, d), lambda i, j: (i, 0)),
            pl.BlockSpec((d, tt), lambda i, j: (0, i)),
            pl.BlockSpec((eb, d), lambda i, j: (j, 0)),
            pl.BlockSpec((d, eb), lambda i, j: (0, j)),
            fac_spec, fac_spec, fac_spec, fac_spec,
        ],
        out_specs=pl.BlockSpec((tt, d), lambda i, j: (i, 0)),
        out_shape=jax.ShapeDtypeStruct((n, d), _F32),
        scratch_shapes=[pltpu.VMEM((d, tt), _F32)],
        compiler_params=pltpu.CompilerParams(dimension_semantics=("arbitrary", "arbitrary"),
                                             vmem_limit_bytes=VMEM_BYTES_V7X * 3 // 4),
        name="peer_dense",
    )(x, ht, u_bf, vt_bf, p1, c1, p2, r2)


def peer_residual(x, g, w_q, sub_keys, u_tab, v_tab):
    ht, p1, c1, p2, r2 = _peer_select(x, g, w_q, sub_keys)
    return _peer_dense(x, ht, u_tab.astype(_BF16), v_tab.T.astype(_BF16), p1, c1, p2, r2)


def kernel(x_prompt, x_sample, cache_a_cmp_kv, cache_a_sel_kv, cache_a_win_kv, state_b_conv, state_c_conv,
           cache_mem_kv, page_table, mem_prompt, norm_mix_g, norm_mem_g, w_mem_kv, mem_q_norm_g, mem_k_norm_g,
           w_out, norm_ffn_g, peer_w_q, peer_sub_keys, peer_u, peer_v, a_w_in, a_q_norm_g, a_k_norm_g,
           a_cmp_pe, a_cmp_w1, a_cmp_b1, a_cmp_w2, b_w_in, b_conv_w, b_conv_b, c_w_in, c_conv_w, c_conv_b,
           c_ln_g, c_ln_b):
    xp, xs = x_prompt, x_sample
    n_p, n_s = xp.shape[0], xs.shape[0]
    n_tok_p = n_p * xp.shape[1]
    p_cmp, p_sel, p_win, p_cb, p_cc, p_mem = [], [], [], [], [], []
    s_cmp, s_sel, s_win, s_cb, s_cc = [], [], [], [], []
    for i in range(DEPTH):
        kind, li = i % N_MIXERS, i // N_MIXERS
        hp = rms_norm(xp, norm_mix_g[i])
        hs = rms_norm(xs, norm_mix_g[i])
        if kind == 0:
            zp = hp @ a_w_in[li]
            zs = hs @ a_w_in[li]
            nsa_w = (a_q_norm_g[li], a_k_norm_g[li], a_cmp_pe[li], a_cmp_w1[li], a_cmp_b1[li], a_cmp_w2[li])
            mp, c_p, sl_p, w_p = nsa_prompt(zp, *nsa_w)
            ms, c_s, sl_s, w_s = nsa_sample(zs, cache_a_cmp_kv[li], cache_a_sel_kv[li], cache_a_win_kv[li],
                                            page_table, *nsa_w)
            p_cmp.append(c_p)
            p_sel.append(sl_p)
            p_win.append(w_p)
            s_cmp.append(c_s)
            s_sel.append(sl_s)
            s_win.append(w_s)
        elif kind == 1:
            zp = hp @ b_w_in[li]
            zs = hs @ b_w_in[li]
            mp, st_p = shortconv_mix(zp, jnp.zeros((n_p, CONV_B_WIDTH - 1, MIX_W), zp.dtype), b_conv_w[li], b_conv_b[li])
            ms, st_s = shortconv_mix(zs, state_b_conv[li], b_conv_w[li], b_conv_b[li])
            p_cb.append(st_p)
            s_cb.append(st_s)
        else:
            zp = hp @ c_w_in[li]
            zs = hs @ c_w_in[li]
            mp, st_p = conformer_mix(zp, jnp.zeros((n_p, CONV_C_WIDTH - 1, MIX_W), zp.dtype),
                                     c_conv_w[li], c_conv_b[li], c_ln_g[li], c_ln_b[li])
            ms, st_s = conformer_mix(zs, state_c_conv[li], c_conv_w[li], c_conv_b[li], c_ln_g[li], c_ln_b[li])
            p_cc.append(st_p)
            s_cc.append(st_s)
        mkv_p = memory_kv(mem_prompt, norm_mem_g[i], w_mem_kv[i], mem_k_norm_g[i])
        p_mem.append(mkv_p)
        op = jnp.concatenate([mp, memory_attend(zp[..., -MEM_W:], mkv_p, mem_q_norm_g[i])], axis=-1)
        os_ = jnp.concatenate([ms, memory_attend(zs[..., -MEM_W:], cache_mem_kv[i], mem_q_norm_g[i])], axis=-1)
        xp = xp + op @ w_out[i]
        xs = xs + os_ @ w_out[i]
        x_all = jnp.concatenate([xp.reshape(-1, D_MODEL), xs.reshape(-1, D_MODEL)], axis=0)
        x_all = peer_residual(x_all, norm_ffn_g[i], peer_w_q[i], peer_sub_keys[i], peer_u[i], peer_v[i])
        xp = x_all[:n_tok_p].reshape(xp.shape)
        xs = x_all[n_tok_p:].reshape(xs.shape)
    return (xp, xs, jnp.stack(p_cmp), jnp.stack(p_sel), jnp.stack(p_win), jnp.stack(p_cb), jnp.stack(p_cc),
            jnp.stack(p_mem), jnp.stack(s_cmp), jnp.stack(s_sel), jnp.stack(s_win), jnp.stack(s_cb), jnp.stack(s_cc))
```

```python
import functools

import numpy as np
import jax
import jax.numpy as jnp
from jax import lax
from jax.experimental import pallas as pl
from jax.experimental.pallas import tpu as pltpu

D_MODEL = 1024
BATCH = 2
SEQ = 8192
DEPTH = 4
DEC_BATCH = 128
DEC_SEQ = 8
PAST_LEN = 2048
PAGE_SIZE = 128

N_MIXERS = 3
HEAD_DIM = 64
MIX_W = 3 * D_MODEL // 4
MEM_W = D_MODEL - MIX_W
MEM_HEADS = MEM_W // HEAD_DIM
N_MEM = 256
NSA_HEADS = MIX_W // HEAD_DIM
NSA_KV_HEADS = 4
NSA_GROUP = NSA_HEADS // NSA_KV_HEADS
NSA_KV_W = 2 * NSA_KV_HEADS * HEAD_DIM
CMP_STRIDE = 16
CMP_LEN = 2 * CMP_STRIDE
CMP_HIDDEN = 64
SEL_BLOCK = 64
N_SEL = 16
WINDOW = 512
NSA_QBLOCK = 128
FORCE_BONUS = 1.0e4
A_IN_W = MIX_W + 3 * NSA_KV_W + 3 * NSA_HEADS + MEM_W
CONV_B_WIDTH = 3
B_IN_W = 3 * MIX_W + MEM_W
CONV_C_WIDTH = 31
C_IN_W = 2 * MIX_W + MEM_W
PEER_HEADS = 8
PEER_KEYS = 128
N_EXPERTS = PEER_KEYS * PEER_KEYS
PEER_TOPK = 16
PEER_DK = 256
EPS = 1e-6
NEG = -1e30

LANES = 128
VMEM_BYTES_V7X = 64 * 1024 * 1024

_F32 = jnp.float32
_BF16 = jnp.bfloat16
_HI = lax.Precision.HIGHEST


def rms_norm(x, g):
    xf = x.astype(jnp.float32)
    y = xf * lax.rsqrt(jnp.mean(xf * xf, axis=-1, keepdims=True) + EPS)
    return (y * g.astype(jnp.float32)).astype(x.dtype)


def layer_norm(x, g, b):
    xf = x.astype(jnp.float32)
    mu = jnp.mean(xf, axis=-1, keepdims=True)
    var = jnp.mean(jnp.square(xf - mu), axis=-1, keepdims=True)
    y = (xf - mu) * lax.rsqrt(var + EPS)
    return (y * g.astype(jnp.float32) + b.astype(jnp.float32)).astype(x.dtype)


def masked_softmax(s, mask):
    p = jax.nn.softmax(jnp.where(mask, s.astype(jnp.float32), NEG), axis=-1)
    return jnp.where(mask, p, 0.0)


def causal_dwconv(u_ext, w, b):
    y = lax.conv_general_dilated(u_ext, w[:, None, :].astype(u_ext.dtype), window_strides=(1,),
                                 padding="VALID", dimension_numbers=("NWC", "WIO", "NWC"),
                                 feature_group_count=u_ext.shape[-1])
    return y + b


def memory_kv(mem, g_norm, w_kv, k_g):
    n, m, _ = mem.shape
    kv = (rms_norm(mem, g_norm) @ w_kv).reshape(n, m, 2, MEM_HEADS, HEAD_DIM)
    return jnp.stack([rms_norm(kv[:, :, 0], k_g), kv[:, :, 1]], axis=2)


def memory_attend(zq, mkv, q_g):
    n, t, _ = zq.shape
    q = rms_norm(zq.reshape(n, t, MEM_HEADS, HEAD_DIM), q_g)
    s = jnp.einsum("nthd,nmhd->nhtm", q, mkv[:, :, 0]) * (HEAD_DIM ** -0.5)
    p = jax.nn.softmax(s.astype(jnp.float32), axis=-1).astype(zq.dtype)
    return jnp.einsum("nhtm,nmhd->nthd", p, mkv[:, :, 1]).reshape(n, t, MEM_W)


def nsa_split(z, q_g, k_g):
    n, t, _ = z.shape
    G, R, HD = NSA_KV_HEADS, NSA_GROUP, HEAD_DIM
    q = rms_norm(z[..., :MIX_W].reshape(n, t, G, R, HD), q_g)
    o = MIX_W
    cmp_kv = z[..., o:o + NSA_KV_W].reshape(n, t, 2, G, HD)
    sel = z[..., o + NSA_KV_W:o + 2 * NSA_KV_W].reshape(n, t, 2, G, HD)
    win = z[..., o + 2 * NSA_KV_W:o + 3 * NSA_KV_W].reshape(n, t, 2, G, HD)
    sel_kv = jnp.stack([rms_norm(sel[:, :, 0], k_g[1]), sel[:, :, 1]], axis=2)
    win_kv = jnp.stack([rms_norm(win[:, :, 0], k_g[2]), win[:, :, 1]], axis=2)
    o = o + 3 * NSA_KV_W
    gates = jax.nn.sigmoid(z[..., o:o + 3 * NSA_HEADS]).reshape(n, t, G, R, 3)
    return q, cmp_kv, sel_kv, win_kv, gates


def compress(rows, pe, w1, b1, w2):
    n, length, G, HD = rows.shape
    n_ch = length // CMP_STRIDE
    ch = rows[:, :n_ch * CMP_STRIDE].reshape(n, n_ch, CMP_STRIDE, G, HD)
    a = jnp.einsum("ncsgd,ksdh->nckgh", ch, w1.reshape(2, CMP_STRIDE, HD, CMP_HIDDEN))
    bias = pe.reshape(-1) @ w1 + b1
    pre = a[:, :-1, 0] + a[:, 1:, 1] + bias
    return jax.nn.gelu(pre, approximate=False) @ w2


def selection_map(n_cmp, n_sel):
    start = np.arange(n_cmp)[:, None] * CMP_STRIDE
    blk = np.arange(n_sel)[None, :] * SEL_BLOCK
    m = (start <= blk + SEL_BLOCK - 1) & (start + CMP_LEN - 1 >= blk)
    return jnp.asarray(m.astype(np.float32))


def nsa_block(q, q_pos, kc, vc, c_end, ks, vs, kw, vw, w_pos, gates, sel_map):
    scale = HEAD_DIM ** -0.5
    tq = q.shape[0]
    n_sel = sel_map.shape[1]
    m_c = (c_end[None, :] <= q_pos[:, None])[:, None, None, :]
    p_c = masked_softmax(jnp.einsum("tgrd,ngd->tgrn", q, kc) * scale, m_c)
    o_c = jnp.einsum("tgrn,ngd->tgrd", p_c.astype(vc.dtype), vc)
    imp = jnp.einsum("tgn,ns->tgs", jnp.sum(p_c, axis=2), sel_map)
    blk = jnp.arange(n_sel)[None, :]
    cur = (q_pos // SEL_BLOCK)[:, None]
    forced = (blk == 0) | (blk == cur) | (blk == cur - 1)
    valid = blk * SEL_BLOCK <= q_pos[:, None]
    bonus = jnp.where(forced, FORCE_BONUS, 0.0)
    imp = jnp.where(valid[:, None, :], imp + bonus[:, None, :], NEG)
    _, idx = lax.top_k(imp, min(N_SEL, n_sel))
    k_eff = idx.shape[-1]
    g_idx = jnp.arange(NSA_KV_HEADS)[None, :, None]

    def gather_blocks(a):
        a = a.reshape(n_sel, SEL_BLOCK, NSA_KV_HEADS, HEAD_DIM).transpose(2, 0, 1, 3)
        return a[g_idx, idx].reshape(tq, NSA_KV_HEADS, k_eff * SEL_BLOCK, HEAD_DIM)

    k_s = gather_blocks(ks)
    v_s = gather_blocks(vs)
    pos_s = (idx[..., None] * SEL_BLOCK + jnp.arange(SEL_BLOCK)).reshape(tq, NSA_KV_HEADS, k_eff * SEL_BLOCK)
    m_s = (pos_s <= q_pos[:, None, None])[:, :, None, :]
    p_s = masked_softmax(jnp.einsum("tgrd,tgld->tgrl", q, k_s) * scale, m_s)
    o_s = jnp.einsum("tgrl,tgld->tgrd", p_s.astype(v_s.dtype), v_s)
    dist = q_pos[:, None] - w_pos[None, :]
    m_w = ((dist >= 0) & (dist <= WINDOW) & (w_pos[None, :] >= 0))[:, None, None, :]
    p_w = masked_softmax(jnp.einsum("tgrd,lgd->tgrl", q, kw) * scale, m_w)
    o_w = jnp.einsum("tgrl,lgd->tgrd", p_w.astype(vw.dtype), vw)
    return gates[..., 0:1] * o_c + gates[..., 1:2] * o_s + gates[..., 2:3] * o_w


def nsa_prompt(z, q_g, k_g, pe, w1, b1, w2):
    n, s, _ = z.shape
    q, cmp_kv, sel_kv, win_kv, gates = nsa_split(z, q_g, k_g)
    kc = rms_norm(compress(cmp_kv[:, :, 0], pe[0], w1[0], b1[0], w2[0]), k_g[0])
    vc = compress(cmp_kv[:, :, 1], pe[1], w1[1], b1[1], w2[1])
    out = nsa_prompt_attend(q, kc, vc, sel_kv, win_kv, gates)
    pages = lambda a: a.reshape(n, s // PAGE_SIZE, PAGE_SIZE, 2, NSA_KV_HEADS, HEAD_DIM)
    wb = min(WINDOW, s)
    return out, pages(cmp_kv), pages(sel_kv), win_kv[:, s - wb:]


_NSA_SEL_TILE = 512


def _softmax_rows(s, mask):
    sm = jnp.where(mask, s, NEG)
    e = jnp.exp(sm - jnp.max(sm, axis=-1, keepdims=True))
    return jnp.where(mask, e / jnp.sum(e, axis=-1, keepdims=True), 0.0)


def _nsa_prompt_kernel(q_ref, gate_ref, kct_ref, vc_ref, smap_ref, kst_ref, vs_ref, kwt_ref, vw_ref, o_ref,
                       m_scr, l_scr, acc_scr):
    qb = pl.program_id(1)
    tq = q_ref.shape[3]
    n_cmp = kct_ref.shape[3]
    n_sel = smap_ref.shape[1]
    tk = kst_ref.shape[4]
    n_top = min(N_SEL, n_sel)
    scale = HEAD_DIM ** -0.5
    s0 = qb * tq
    pos = s0 + lax.broadcasted_iota(jnp.int32, (tq, 1), 0)
    pos3 = jnp.concatenate([pos] * NSA_GROUP, axis=0)
    gates = gate_ref[0]
    cend = lax.broadcasted_iota(jnp.int32, (1, n_cmp), 1) * CMP_STRIDE + (CMP_LEN - 1)
    blk = lax.broadcasted_iota(jnp.int32, (1, n_sel), 1)
    cur = pos // SEL_BLOCK
    forced = (blk == 0) | (blk == cur) | (blk == cur - 1)
    valid = blk * SEL_BLOCK <= pos
    n_tiles = (s0 + tq + tk - 1) // tk
    n_wt = WINDOW // tq + 1
    wt0 = jnp.maximum(qb - WINDOW // tq, 0)
    kposw = wt0 * tq + lax.broadcasted_iota(jnp.int32, (1, n_wt * tq), 1)
    dist = pos3 - kposw
    mw = (dist >= 0) & (dist <= WINDOW)

    for g in range(NSA_KV_HEADS):
        cols = slice(g * HEAD_DIM, (g + 1) * HEAD_DIM)
        qg = q_ref[0, g].reshape(NSA_GROUP * tq, HEAD_DIM)
        qg_bf = qg.astype(_BF16)
        sc = jnp.dot(qg, kct_ref[0, g], precision=_HI, preferred_element_type=_F32) * scale
        p = _softmax_rows(sc, cend <= pos3)
        oc = jnp.dot(p.astype(_BF16), vc_ref[0, g], preferred_element_type=_F32)
        psum = p[0:tq]
        for r in range(1, NSA_GROUP):
            psum = psum + p[r * tq:(r + 1) * tq]
        imp = jnp.dot(psum, smap_ref[...], precision=_HI, preferred_element_type=_F32)
        imp = jnp.where(valid, imp + jnp.where(forced, FORCE_BONUS, 0.0), NEG)
        _, rank_t = _top_rows(imp.T, n_top)
        sel = (rank_t < float(n_top)).astype(_F32).T.astype(_BF16)
        m_scr[...] = jnp.full(m_scr.shape, NEG, _F32)
        l_scr[...] = jnp.zeros(l_scr.shape, _F32)
        acc_scr[...] = jnp.zeros(acc_scr.shape, _F32)

        def body(kt, carry):
            s = jnp.dot(qg_bf, kst_ref[0, g, kt], preferred_element_type=_F32) * scale
            brow = lax.broadcasted_iota(jnp.int32, (n_sel, tk), 0)
            bkey = kt * (tk // SEL_BLOCK) + lax.broadcasted_iota(jnp.int32, (n_sel, tk), 1) // SEL_BLOCK
            selx = jnp.dot(sel, (brow == bkey).astype(_BF16), preferred_element_type=_F32)
            kpos = kt * tk + lax.broadcasted_iota(jnp.int32, (1, tk), 1)
            mk = jnp.where((selx > 0.5) & (kpos <= pos), 1.0, 0.0)
            mk3 = jnp.concatenate([mk] * NSA_GROUP, axis=0) > 0.5
            sm = jnp.where(mk3, s, NEG)
            m_old = m_scr[...]
            m_new = jnp.maximum(m_old, jnp.max(sm, axis=-1, keepdims=True))
            alpha = jnp.exp(m_old - m_new)
            pe = jnp.where(mk3, jnp.exp(sm - m_new), 0.0)
            l_scr[...] = alpha * l_scr[...] + jnp.sum(pe, axis=-1, keepdims=True)
            acc_scr[...] = alpha * acc_scr[...] + jnp.dot(pe.astype(_BF16), vs_ref[0, kt],
                                                          preferred_element_type=_F32)
            m_scr[...] = m_new
            return carry

        lax.fori_loop(0, n_tiles, body, 0)
        osel = acc_scr[:, cols] / l_scr[...]
        kwt = jnp.concatenate([kwt_ref[0, g, wt0 + i] for i in range(n_wt)], axis=1)
        vw = jnp.concatenate([vw_ref[0, wt0 + i] for i in range(n_wt)], axis=0)
        sw = jnp.dot(qg_bf, kwt, preferred_element_type=_F32) * scale
        pw = _softmax_rows(sw, mw)
        ow = jnp.dot(pw.astype(_BF16), vw, preferred_element_type=_F32)[:, cols]
        for r in range(NSA_GROUP):
            c = (g * NSA_GROUP + r) * 3
            rows = slice(r * tq, (r + 1) * tq)
            o_ref[0, g, r] = (gates[:, c:c + 1] * oc[rows] + gates[:, c + 1:c + 2] * osel[rows]
                              + gates[:, c + 2:c + 3] * ow[rows])


def nsa_prompt_attend(q, kc, vc, sel_kv, win_kv, gates):
    n, s = q.shape[:2]
    G, R, HD = NSA_KV_HEADS, NSA_GROUP, HEAD_DIM
    tq, tk = NSA_QBLOCK, _NSA_SEL_TILE
    n_cmp = kc.shape[1]
    ncp = s // CMP_STRIDE
    n_sel = s // SEL_BLOCK
    assert s % tk == 0 and s >= WINDOW + tq
    padc = ((0, 0), (0, ncp - n_cmp), (0, 0), (0, 0))
    qh = q.transpose(0, 2, 3, 1, 4)
    kct = jnp.pad(kc, padc).transpose(0, 2, 3, 1)
    vcb = jnp.pad(vc, padc).transpose(0, 2, 1, 3).astype(_BF16)
    smap = jnp.pad(selection_map(n_cmp, n_sel), ((0, ncp - n_cmp), (0, 0)))
    kst = sel_kv[:, :, 0].astype(_BF16).reshape(n, s // tk, tk, G, HD).transpose(0, 3, 1, 4, 2)
    vs = sel_kv[:, :, 1].astype(_BF16).reshape(n, s // tk, tk, G * HD)
    kwt = win_kv[:, :, 0].astype(_BF16).reshape(n, s // tq, tq, G, HD).transpose(0, 3, 1, 4, 2)
    vw = win_kv[:, :, 1].astype(_BF16).reshape(n, s // tq, tq, G * HD)
    once = pl.Buffered(1)
    out = pl.pallas_call(
        _nsa_prompt_kernel,
        grid=(n, s // tq),
        in_specs=[
            pl.BlockSpec((1, G, R, tq, HD), lambda b, i: (b, 0, 0, i, 0)),
            pl.BlockSpec((1, tq, G * R * 3), lambda b, i: (b, i, 0)),
            pl.BlockSpec((1, G, HD, ncp), lambda b, i: (b, 0, 0, 0)),
            pl.BlockSpec((1, G, ncp, HD), lambda b, i: (b, 0, 0, 0)),
            pl.BlockSpec((ncp, n_sel), lambda b, i: (0, 0)),
            pl.BlockSpec((1, G, s // tk, HD, tk), lambda b, i: (b, 0, 0, 0, 0), pipeline_mode=once),
            pl.BlockSpec((1, s // tk, tk, G * HD), lambda b, i: (b, 0, 0, 0), pipeline_mode=once),
            pl.BlockSpec((1, G, s // tq, HD, tq), lambda b, i: (b, 0, 0, 0, 0), pipeline_mode=once),
            pl.BlockSpec((1, s // tq, tq, G * HD), lambda b, i: (b, 0, 0, 0), pipeline_mode=once),
        ],
        out_specs=pl.BlockSpec((1, G, R, tq, HD), lambda b, i: (b, 0, 0, i, 0)),
        out_shape=jax.ShapeDtypeStruct((n, G, R, s, HD), _F32),
        scratch_shapes=[pltpu.VMEM((R * tq, 1), _F32), pltpu.VMEM((R * tq, 1), _F32),
                        pltpu.VMEM((R * tq, G * HD), _F32)],
        compiler_params=pltpu.CompilerParams(dimension_semantics=("arbitrary", "arbitrary"),
                                             vmem_limit_bytes=VMEM_BYTES_V7X * 3 // 4),
        name="nsa_prompt",
    )(qh, gates.reshape(n, s, G * R * 3), kct, vcb, smap, kst, vs, kwt, vw)
    return out.transpose(0, 3, 1, 2, 4).reshape(n, s, MIX_W)


def _copy_kernel(pt_ref, src_ref, dst_ref):
    dst_ref[0] = src_ref[...]


def _gather_pages(pool, page_table):
    n, n_pages = page_table.shape
    w = int(np.prod(pool.shape[2:]))
    flat = pool.reshape(pool.shape[0], PAGE_SIZE, w)
    out = pl.pallas_call(
        _copy_kernel,
        grid_spec=pltpu.PrefetchScalarGridSpec(
            num_scalar_prefetch=1,
            grid=(n, n_pages),
            in_specs=[pl.BlockSpec((1, PAGE_SIZE, w), lambda b, p, pt: (pt[b, p], 0, 0))],
            out_specs=pl.BlockSpec((1, 1, PAGE_SIZE, w), lambda b, p, pt: (b, p, 0, 0)),
        ),
        out_shape=jax.ShapeDtypeStruct((n, n_pages, PAGE_SIZE, w), pool.dtype),
        name="gather_pages",
    )(page_table, flat)
    return out.reshape((n, n_pages) + pool.shape[1:])


def nsa_sample(z, pool_cmp, pool_sel, win_buf, page_table, q_g, k_g, pe, w1, b1, w2):
    n, t_new, _ = z.shape
    past = page_table.shape[1] * PAGE_SIZE
    q, cmp_new, sel_new, win_new, gates = nsa_split(z, q_g, k_g)
    gather = lambda pool: _gather_pages(pool, page_table).reshape(n, past, 2, NSA_KV_HEADS, HEAD_DIM)
    cmp_all = jnp.concatenate([gather(pool_cmp), cmp_new], axis=1)
    total = past + t_new
    n_sel = -(-total // SEL_BLOCK)
    sel_all = jnp.pad(jnp.concatenate([gather(pool_sel), sel_new], axis=1),
                      ((0, 0), (0, n_sel * SEL_BLOCK - total), (0, 0), (0, 0), (0, 0)))
    win_all = jnp.concatenate([win_buf, win_new], axis=1)
    wb = win_buf.shape[1]
    kc = rms_norm(compress(cmp_all[:, :, 0], pe[0], w1[0], b1[0], w2[0]), k_g[0])
    vc = compress(cmp_all[:, :, 1], pe[1], w1[1], b1[1], w2[1])
    n_cmp = kc.shape[1]
    c_end = jnp.arange(n_cmp) * CMP_STRIDE + (CMP_LEN - 1)
    sel_map = selection_map(n_cmp, n_sel)
    q_pos = past + jnp.arange(t_new)
    w_pos = past - wb + jnp.arange(wb + t_new)

    def step(a):
        qq, kcc, vcc, kss, vss, kww, vww, gg = a
        return nsa_block(qq, q_pos, kcc, vcc, c_end, kss, vss, kww, vww, w_pos, gg, sel_map)

    out = lax.map(step, (q, kc, vc, sel_all[:, :, 0], sel_all[:, :, 1], win_all[:, :, 0], win_all[:, :, 1], gates))
    return out.reshape(n, t_new, MIX_W), cmp_new, sel_new, win_all[:, -wb:]


def shortconv_mix(z, ctx, w, b):
    bg = z[..., :MIX_W]
    cg = z[..., MIX_W:2 * MIX_W]
    v = z[..., 2 * MIX_W:3 * MIX_W]
    u_ext = jnp.concatenate([ctx, cg * v], axis=1)
    y = bg * causal_dwconv(u_ext, w, b)
    return y, u_ext[:, -(CONV_B_WIDTH - 1):]


def conformer_mix(z, ctx, w, b, ln_g, ln_b):
    u = z[..., :MIX_W] * jax.nn.sigmoid(z[..., MIX_W:2 * MIX_W])
    u_ext = jnp.concatenate([ctx, u], axis=1)
    y = jax.nn.silu(layer_norm(causal_dwconv(u_ext, w, b), ln_g, ln_b))
    return y, u_ext[:, -(CONV_C_WIDTH - 1):]


_CAND = [(j1, j2) for j1 in range(PEER_TOPK) for j2 in range(PEER_TOPK) if (j1 + 1) * (j2 + 1) <= PEER_TOPK]
_N_CAND = len(_CAND)
_CAND_ROWS = -(-_N_CAND // 8) * 8
_PEER_SEL_TILE = 256
_PEER_TOK_TILE = 512
_PEER_EXP_BLOCK = 1024
_PEER_SUB_BLOCK = 256


def _cand_tables():
    j1 = np.full((_CAND_ROWS, LANES), -1.0, np.float32)
    j2 = np.full((_CAND_ROWS, LANES), -1.0, np.float32)
    flat = np.full((_CAND_ROWS, LANES), 1.0e6, np.float32)
    for r, (a, b) in enumerate(_CAND):
        j1[r], j2[r], flat[r] = a, b, a * PEER_TOPK + b
    return np.stack([j1, j2, flat])


def _gelu_erf(x):
    return 0.5 * x * (1.0 + lax.erf(x * (2.0 ** -0.5)))


def _top_rows(s, n_top):
    k, l = s.shape
    kidx = lax.broadcasted_iota(jnp.int32, (k, l), 0).astype(_F32)
    jrow = lax.broadcasted_iota(jnp.int32, (n_top, l), 0)
    work = s
    rank = jnp.full((k, l), float(n_top), _F32)
    vals = jnp.zeros((n_top, l), _F32)
    for j in range(n_top):
        m = jnp.max(work, axis=0, keepdims=True)
        first = jnp.min(jnp.where(work == m, kidx, float(k)), axis=0, keepdims=True)
        sel = kidx == first
        rank = jnp.where(sel, float(j), rank)
        work = jnp.where(sel, -jnp.inf, work)
        vals = jnp.where(jrow == j, m, vals)
    return vals, rank


def _peer_select_kernel(x_ref, g_ref, wq_ref, sk_ref, ct_ref, ht_ref, p1_ref, c1_ref, p2_ref, r2_ref, h_scr):
    hd = pl.program_id(1)

    @pl.when(hd == 0)
    def _():
        x = x_ref[...]
        h = x * lax.rsqrt(jnp.mean(x * x, axis=-1, keepdims=True) + EPS) * g_ref[...]
        h_scr[...] = h
        ht_ref[...] = h.T.astype(_BF16)

    q = jnp.dot(h_scr[...], wq_ref[...], precision=_HI, preferred_element_type=_F32)
    nt = (((1,), (1,)), ((), ()))
    half = PEER_DK // 2
    s1_all = lax.dot_general(sk_ref[0, 0], q[:, :half], nt, precision=_HI, preferred_element_type=_F32)
    s2_all = lax.dot_general(sk_ref[0, 1], q[:, half:], nt, precision=_HI, preferred_element_type=_F32)
    cj1, cj2, cflat = ct_ref[0], ct_ref[1], ct_ref[2]
    for c in range(x_ref.shape[0] // LANES):
        sl = slice(c * LANES, (c + 1) * LANES)
        s1, s2 = s1_all[:, sl], s2_all[:, sl]
        v1, rank1 = _top_rows(s1, PEER_TOPK)
        v2, rank2 = _top_rows(s2, PEER_TOPK)
        a1 = jnp.zeros((_CAND_ROWS, LANES), _F32)
        a2 = jnp.zeros((_CAND_ROWS, LANES), _F32)
        for j in range(PEER_TOPK):
            a1 = jnp.where(cj1 == float(j), v1[j:j + 1, :], a1)
            a2 = jnp.where(cj2 == float(j), v2[j:j + 1, :], a2)
        cand = jnp.where(cj1 >= 0.0, a1 + a2, -jnp.inf)
        work = cand
        picked = jnp.zeros((_CAND_ROWS, LANES), _F32)
        for _ in range(PEER_TOPK):
            m = jnp.max(work, axis=0, keepdims=True)
            first = jnp.min(jnp.where(work == m, cflat, 2.0e6), axis=0, keepdims=True)
            sel = cflat == first
            picked = jnp.where(sel, 1.0, picked)
            work = jnp.where(sel, -jnp.inf, work)
        top = v1[0:1, :] + v2[0:1, :]
        z = jnp.sum(jnp.where(picked > 0.0, jnp.exp(cand - top), 0.0), axis=0, keepdims=True)
        c1 = jnp.zeros((PEER_KEYS, LANES), _F32)
        for j in range(PEER_TOPK):
            cnt_j = jnp.sum(jnp.where(cj1 == float(j), picked, 0.0), axis=0, keepdims=True)
            c1 = jnp.where(rank1 == float(j), cnt_j, c1)
        p1_ref[0, :, sl] = jnp.exp(s1 - v1[0:1, :]) / z
        c1_ref[0, :, sl] = c1
        p2_ref[0, :, sl] = jnp.exp(s2 - v2[0:1, :])
        r2_ref[0, :, sl] = rank2


def _peer_select(x, g, w_q, sub_keys):
    n, d = x.shape
    ts = _PEER_SEL_TILE
    assert n % ts == 0
    fac = jax.ShapeDtypeStruct((PEER_HEADS, PEER_KEYS, n), _F32)
    fac_spec = pl.BlockSpec((1, PEER_KEYS, ts), lambda i, h: (h, 0, i))
    return pl.pallas_call(
        _peer_select_kernel,
        grid=(n // ts, PEER_HEADS),
        in_specs=[
            pl.BlockSpec((ts, d), lambda i, h: (i, 0)),
            pl.BlockSpec((1, d), lambda i, h: (0, 0)),
            pl.BlockSpec((d, PEER_DK), lambda i, h: (0, h)),
            pl.BlockSpec((1, 2, PEER_KEYS, PEER_DK // 2), lambda i, h: (h, 0, 0, 0)),
            pl.BlockSpec((3, _CAND_ROWS, LANES), lambda i, h: (0, 0, 0)),
        ],
        out_specs=[pl.BlockSpec((d, ts), lambda i, h: (0, i)), fac_spec, fac_spec, fac_spec, fac_spec],
        out_shape=[jax.ShapeDtypeStruct((d, n), _BF16), fac, fac, fac, fac],
        scratch_shapes=[pltpu.VMEM((ts, d), _F32)],
        compiler_params=pltpu.CompilerParams(dimension_semantics=("arbitrary", "arbitrary"),
                                             vmem_limit_bytes=VMEM_BYTES_V7X * 3 // 4),
        name="peer_select",
    )(x, g.reshape(1, d), w_q, sub_keys, jnp.asarray(_cand_tables()))


def _peer_dense_kernel(x_ref, ht_ref, u_ref, vt_ref, p1_ref, c1_ref, p2_ref, r2_ref, o_ref,
                       acc_ref, *scratch):
    j = pl.program_id(1)

    @pl.when(j == 0)
    def _():
        acc_ref[...] = jnp.zeros_like(acc_ref)

    eb, tt = u_ref.shape[0], ht_ref.shape[1]
    sub = _PEER_SUB_BLOCK
    n_sub = eb // sub
    st_refs, wact_refs = scratch[:2], scratch[2:]
    half = tt // 2
    d = vt_ref.shape[0]
    n_out = d // sub

    def scores(k, nh):
        lanes = slice(nh * half, (nh + 1) * half)
        st_refs[k % 2][:, lanes] = jnp.dot(u_ref[k * sub:(k + 1) * sub, :], ht_ref[:, lanes],
                                           preferred_element_type=_F32)

    def output_piece(k, q):
        r = slice(q * sub, (q + 1) * sub)
        acc_ref[r, :] += jnp.dot(vt_ref[r, k * sub:(k + 1) * sub], wact_refs[k % 2][...],
                                 preferred_element_type=_F32)

    def build_chunk(k, al, c, a):
        brows = slice(al * PEER_KEYS, (al + 1) * PEER_KEYS)
        lanes = slice(c * LANES, (c + 1) * LANES)
        w = jnp.zeros((PEER_KEYS, LANES), _F32)
        for h in range(PEER_HEADS):
            c1 = c1_ref[h, pl.ds(a, 1), :][:, lanes]
            p1 = p1_ref[h, pl.ds(a, 1), :][:, lanes]
            w = w + jnp.where(r2_ref[h, :, lanes] < c1, p2_ref[h, :, lanes], 0.0) * p1
        wact_refs[k % 2][brows, lanes] = (w * _gelu_erf(st_refs[k % 2][brows, lanes])).astype(_BF16)

    n_al, n_c = sub // PEER_KEYS, tt // LANES
    n_chunks = n_al * n_c
    scores(0, 0)
    scores(0, 1)
    for k in range(n_sub):
        i = 0
        for al in range(n_al):
            a = j * (eb // PEER_KEYS) + k * n_al + al
            for c in range(n_c):
                build_chunk(k, al, c, a)
                if k + 1 < n_sub and i % (n_chunks // 2) == 0:
                    scores(k + 1, i // (n_chunks // 2))
                if k >= 1 and i % (n_chunks // n_out) == n_chunks // n_out - 1:
                    output_piece(k - 1, i // (n_chunks // n_out))
                i += 1
    for q in range(n_out):
        output_piece(n_sub - 1, q)

    @pl.when(j == pl.num_programs(1) - 1)
    def _():
        o_ref[...] = x_ref[...] + acc_ref[...].T


def _peer_dense(x, ht, u_bf, vt_bf, p1, c1, p2, r2):
    n, d = x.shape
    tt, eb = _PEER_TOK_TILE, _PEER_EXP_BLOCK
    assert n % tt == 0 and N_EXPERTS % eb == 0
    fac_spec = pl.BlockSpec((PEER_HEADS, PEER_KEYS, tt), lambda i, j: (0, 0, i))
    return pl.pallas_call(
        _peer_dense_kernel,
        grid=(n // tt, N_EXPERTS // eb),
        in_specs=[
            pl.BlockSpec((tt, d), lambda i, j: (i, 0)),
            pl.BlockSpec((d, tt), lambda i, j: (0, i)),
            pl.BlockSpec((eb, d), lambda i, j: (j, 0)),
            pl.BlockSpec((d, eb), lambda i, j: (0, j)),
            fac_spec, fac_spec, fac_spec, fac_spec,
        ],
        out_specs=pl.BlockSpec((tt, d), lambda i, j: (i, 0)),
        out_shape=jax.ShapeDtypeStruct((n, d), _F32),
        scratch_shapes=([pltpu.VMEM((d, tt), _F32)]
                        + [pltpu.VMEM((_PEER_SUB_BLOCK, tt), _F32)] * 2
                        + [pltpu.VMEM((_PEER_SUB_BLOCK, tt), _BF16)] * 2),
        compiler_params=pltpu.CompilerParams(dimension_semantics=("arbitrary", "arbitrary"),
                                             vmem_limit_bytes=VMEM_BYTES_V7X * 3 // 4),
        name="peer_dense",
    )(x, ht, u_bf, vt_bf, p1, c1, p2, r2)


def peer_residual(x, g, w_q, sub_keys, u_tab, v_tab):
    ht, p1, c1, p2, r2 = _peer_select(x, g, w_q, sub_keys)
    return _peer_dense(x, ht, u_tab.astype(_BF16), v_tab.T.astype(_BF16), p1, c1, p2, r2)


def kernel(x_prompt, x_sample, cache_a_cmp_kv, cache_a_sel_kv, cache_a_win_kv, state_b_conv, state_c_conv,
           cache_mem_kv, page_table, mem_prompt, norm_mix_g, norm_mem_g, w_mem_kv, mem_q_norm_g, mem_k_norm_g,
           w_out, norm_ffn_g, peer_w_q, peer_sub_keys, peer_u, peer_v, a_w_in, a_q_norm_g, a_k_norm_g,
           a_cmp_pe, a_cmp_w1, a_cmp_b1, a_cmp_w2, b_w_in, b_conv_w, b_conv_b, c_w_in, c_conv_w, c_conv_b,
           c_ln_g, c_ln_b):
    xp, xs = x_prompt, x_sample
    n_p, n_s = xp.shape[0], xs.shape[0]
    n_tok_p = n_p * xp.shape[1]
    p_cmp, p_sel, p_win, p_cb, p_cc, p_mem = [], [], [], [], [], []
    s_cmp, s_sel, s_win, s_cb, s_cc = [], [], [], [], []
    for i in range(DEPTH):
        kind, li = i % N_MIXERS, i // N_MIXERS
        hp = rms_norm(xp, norm_mix_g[i])
        hs = rms_norm(xs, norm_mix_g[i])
        if kind == 0:
            zp = hp @ a_w_in[li]
            zs = hs @ a_w_in[li]
            nsa_w = (a_q_norm_g[li], a_k_norm_g[li], a_cmp_pe[li], a_cmp_w1[li], a_cmp_b1[li], a_cmp_w2[li])
            mp, c_p, sl_p, w_p = nsa_prompt(zp, *nsa_w)
            ms, c_s, sl_s, w_s = nsa_sample(zs, cache_a_cmp_kv[li], cache_a_sel_kv[li], cache_a_win_kv[li],
                                            page_table, *nsa_w)
            p_cmp.append(c_p)
            p_sel.append(sl_p)
            p_win.append(w_p)
            s_cmp.append(c_s)
            s_sel.append(sl_s)
            s_win.append(w_s)
        elif kind == 1:
            zp = hp @ b_w_in[li]
            zs = hs @ b_w_in[li]
            mp, st_p = shortconv_mix(zp, jnp.zeros((n_p, CONV_B_WIDTH - 1, MIX_W), zp.dtype), b_conv_w[li], b_conv_b[li])
            ms, st_s = shortconv_mix(zs, state_b_conv[li], b_conv_w[li], b_conv_b[li])
            p_cb.append(st_p)
            s_cb.append(st_s)
        else:
            zp = hp @ c_w_in[li]
            zs = hs @ c_w_in[li]
            mp, st_p = conformer_mix(zp, jnp.zeros((n_p, CONV_C_WIDTH - 1, MIX_W), zp.dtype),
                                     c_conv_w[li], c_conv_b[li], c_ln_g[li], c_ln_b[li])
            ms, st_s = conformer_mix(zs, state_c_conv[li], c_conv_w[li], c_conv_b[li], c_ln_g[li], c_ln_b[li])
            p_cc.append(st_p)
            s_cc.append(st_s)
        mkv_p = memory_kv(mem_prompt, norm_mem_g[i], w_mem_kv[i], mem_k_norm_g[i])
        p_mem.append(mkv_p)
        op = jnp.concatenate([mp, memory_attend(zp[..., -MEM_W:], mkv_p, mem_q_norm_g[i])], axis=-1)
        os_ = jnp.concatenate([ms, memory_attend(zs[..., -MEM_W:], cache_mem_kv[i], mem_q_norm_g[i])], axis=-1)
        xp = xp + op @ w_out[i]
        xs = xs + os_ @ w_out[i]
        x_all = jnp.concatenate([xp.reshape(-1, D_MODEL), xs.reshape(-1, D_MODEL)], axis=0)
        x_all = peer_residual(x_all, norm_ffn_g[i], peer_w_q[i], peer_sub_keys[i], peer_u[i], peer_v[i])
        xp = x_all[:n_tok_p].reshape(xp.shape)
        xs = x_all[n_tok_p:].reshape(xs.shape)
    return (xp, xs, jnp.stack(p_cmp), jnp.stack(p_sel), jnp.stack(p_win), jnp.stack(p_cb), jnp.stack(p_cc),
            jnp.stack(p_mem), jnp.stack(s_cmp), jnp.stack(s_sel), jnp.stack(s_win), jnp.stack(s_cb), jnp.stack(s_cc))
```

```python
import functools

import numpy as np
import jax
import jax.numpy as jnp
from jax import lax
from jax.experimental import pallas as pl
from jax.experimental.pallas import tpu as pltpu

D_MODEL = 1024
BATCH = 2
SEQ = 8192
DEPTH = 4
DEC_BATCH = 128
DEC_SEQ = 8
PAST_LEN = 2048
PAGE_SIZE = 128

N_MIXERS = 3
HEAD_DIM = 64
MIX_W = 3 * D_MODEL // 4
MEM_W = D_MODEL - MIX_W
MEM_HEADS = MEM_W // HEAD_DIM
N_MEM = 256
NSA_HEADS = MIX_W // HEAD_DIM
NSA_KV_HEADS = 4
NSA_GROUP = NSA_HEADS // NSA_KV_HEADS
NSA_KV_W = 2 * NSA_KV_HEADS * HEAD_DIM
CMP_STRIDE = 16
CMP_LEN = 2 * CMP_STRIDE
CMP_HIDDEN = 64
SEL_BLOCK = 64
N_SEL = 16
WINDOW = 512
NSA_QBLOCK = 128
FORCE_BONUS = 1.0e4
A_IN_W = MIX_W + 3 * NSA_KV_W + 3 * NSA_HEADS + MEM_W
CONV_B_WIDTH = 3
B_IN_W = 3 * MIX_W + MEM_W
CONV_C_WIDTH = 31
C_IN_W = 2 * MIX_W + MEM_W
PEER_HEADS = 8
PEER_KEYS = 128
N_EXPERTS = PEER_KEYS * PEER_KEYS
PEER_TOPK = 16
PEER_DK = 256
EPS = 1e-6
NEG = -1e30

LANES = 128
VMEM_BYTES_V7X = 64 * 1024 * 1024

_F32 = jnp.float32
_BF16 = jnp.bfloat16
_HI = lax.Precision.HIGHEST


def rms_norm(x, g):
    xf = x.astype(jnp.float32)
    y = xf * lax.rsqrt(jnp.mean(xf * xf, axis=-1, keepdims=True) + EPS)
    return (y * g.astype(jnp.float32)).astype(x.dtype)


def layer_norm(x, g, b):
    xf = x.astype(jnp.float32)
    mu = jnp.mean(xf, axis=-1, keepdims=True)
    var = jnp.mean(jnp.square(xf - mu), axis=-1, keepdims=True)
    y = (xf - mu) * lax.rsqrt(var + EPS)
    return (y * g.astype(jnp.float32) + b.astype(jnp.float32)).astype(x.dtype)


def masked_softmax(s, mask):
    p = jax.nn.softmax(jnp.where(mask, s.astype(jnp.float32), NEG), axis=-1)
    return jnp.where(mask, p, 0.0)


def causal_dwconv(u_ext, w, b):
    y = lax.conv_general_dilated(u_ext, w[:, None, :].astype(u_ext.dtype), window_strides=(1,),
                                 padding="VALID", dimension_numbers=("NWC", "WIO", "NWC"),
                                 feature_group_count=u_ext.shape[-1])
    return y + b


def memory_kv(mem, g_norm, w_kv, k_g):
    n, m, _ = mem.shape
    kv = (rms_norm(mem, g_norm) @ w_kv).reshape(n, m, 2, MEM_HEADS, HEAD_DIM)
    return jnp.stack([rms_norm(kv[:, :, 0], k_g), kv[:, :, 1]], axis=2)


def memory_attend(zq, mkv, q_g):
    n, t, _ = zq.shape
    q = rms_norm(zq.reshape(n, t, MEM_HEADS, HEAD_DIM), q_g)
    s = jnp.einsum("nthd,nmhd->nhtm", q, mkv[:, :, 0]) * (HEAD_DIM ** -0.5)
    p = jax.nn.softmax(s.astype(jnp.float32), axis=-1).astype(zq.dtype)
    return jnp.einsum("nhtm,nmhd->nthd", p, mkv[:, :, 1]).reshape(n, t, MEM_W)


def nsa_split(z, q_g, k_g):
    n, t, _ = z.shape
    G, R, HD = NSA_KV_HEADS, NSA_GROUP, HEAD_DIM
    q = rms_norm(z[..., :MIX_W].reshape(n, t, G, R, HD), q_g)
    o = MIX_W
    cmp_kv = z[..., o:o + NSA_KV_W].reshape(n, t, 2, G, HD)
    sel = z[..., o + NSA_KV_W:o + 2 * NSA_KV_W].reshape(n, t, 2, G, HD)
    win = z[..., o + 2 * NSA_KV_W:o + 3 * NSA_KV_W].reshape(n, t, 2, G, HD)
    sel_kv = jnp.stack([rms_norm(sel[:, :, 0], k_g[1]), sel[:, :, 1]], axis=2)
    win_kv = jnp.stack([rms_norm(win[:, :, 0], k_g[2]), win[:, :, 1]], axis=2)
    o = o + 3 * NSA_KV_W
    gates = jax.nn.sigmoid(z[..., o:o + 3 * NSA_HEADS]).reshape(n, t, G, R, 3)
    return q, cmp_kv, sel_kv, win_kv, gates


def compress(rows, pe, w1, b1, w2):
    n, length, G, HD = rows.shape
    n_ch = length // CMP_STRIDE
    ch = rows[:, :n_ch * CMP_STRIDE].reshape(n, n_ch, CMP_STRIDE, G, HD)
    a = jnp.einsum("ncsgd,ksdh->nckgh", ch, w1.reshape(2, CMP_STRIDE, HD, CMP_HIDDEN))
    bias = pe.reshape(-1) @ w1 + b1
    pre = a[:, :-1, 0] + a[:, 1:, 1] + bias
    return jax.nn.gelu(pre, approximate=False) @ w2


def selection_map(n_cmp, n_sel):
    start = np.arange(n_cmp)[:, None] * CMP_STRIDE
    blk = np.arange(n_sel)[None, :] * SEL_BLOCK
    m = (start <= blk + SEL_BLOCK - 1) & (start + CMP_LEN - 1 >= blk)
    return jnp.asarray(m.astype(np.float32))


def nsa_block(q, q_pos, kc, vc, c_end, ks, vs, kw, vw, w_pos, gates, sel_map):
    scale = HEAD_DIM ** -0.5
    tq = q.shape[0]
    n_sel = sel_map.shape[1]
    m_c = (c_end[None, :] <= q_pos[:, None])[:, None, None, :]
    p_c = masked_softmax(jnp.einsum("tgrd,ngd->tgrn", q, kc) * scale, m_c)
    o_c = jnp.einsum("tgrn,ngd->tgrd", p_c.astype(vc.dtype), vc)
    imp = jnp.einsum("tgn,ns->tgs", jnp.sum(p_c, axis=2), sel_map)
    blk = jnp.arange(n_sel)[None, :]
    cur = (q_pos // SEL_BLOCK)[:, None]
    forced = (blk == 0) | (blk == cur) | (blk == cur - 1)
    valid = blk * SEL_BLOCK <= q_pos[:, None]
    bonus = jnp.where(forced, FORCE_BONUS, 0.0)
    imp = jnp.where(valid[:, None, :], imp + bonus[:, None, :], NEG)
    _, idx = lax.top_k(imp, min(N_SEL, n_sel))
    k_eff = idx.shape[-1]
    g_idx = jnp.arange(NSA_KV_HEADS)[None, :, None]

    def gather_blocks(a):
        a = a.reshape(n_sel, SEL_BLOCK, NSA_KV_HEADS, HEAD_DIM).transpose(2, 0, 1, 3)
        return a[g_idx, idx].reshape(tq, NSA_KV_HEADS, k_eff * SEL_BLOCK, HEAD_DIM)

    k_s = gather_blocks(ks)
    v_s = gather_blocks(vs)
    pos_s = (idx[..., None] * SEL_BLOCK + jnp.arange(SEL_BLOCK)).reshape(tq, NSA_KV_HEADS, k_eff * SEL_BLOCK)
    m_s = (pos_s <= q_pos[:, None, None])[:, :, None, :]
    p_s = masked_softmax(jnp.einsum("tgrd,tgld->tgrl", q, k_s) * scale, m_s)
    o_s = jnp.einsum("tgrl,tgld->tgrd", p_s.astype(v_s.dtype), v_s)
    dist = q_pos[:, None] - w_pos[None, :]
    m_w = ((dist >= 0) & (dist <= WINDOW) & (w_pos[None, :] >= 0))[:, None, None, :]
    p_w = masked_softmax(jnp.einsum("tgrd,lgd->tgrl", q, kw) * scale, m_w)
    o_w = jnp.einsum("tgrl,lgd->tgrd", p_w.astype(vw.dtype), vw)
    return gates[..., 0:1] * o_c + gates[..., 1:2] * o_s + gates[..., 2:3] * o_w


def nsa_prompt(z, q_g, k_g, pe, w1, b1, w2):
    n, s, _ = z.shape
    q, cmp_kv, sel_kv, win_kv, gates = nsa_split(z, q_g, k_g)
    kc = rms_norm(compress(cmp_kv[:, :, 0], pe[0], w1[0], b1[0], w2[0]), k_g[0])
    vc = compress(cmp_kv[:, :, 1], pe[1], w1[1], b1[1], w2[1])
    out = nsa_prompt_attend(q, kc, vc, sel_kv, win_kv, gates)
    pages = lambda a: a.reshape(n, s // PAGE_SIZE, PAGE_SIZE, 2, NSA_KV_HEADS, HEAD_DIM)
    wb = min(WINDOW, s)
    return out, pages(cmp_kv), pages(sel_kv), win_kv[:, s - wb:]


_NSA_SEL_TILE = 512


def _softmax_rows(s, mask):
    sm = jnp.where(mask, s, NEG)
    e = jnp.exp(sm - jnp.max(sm, axis=-1, keepdims=True))
    return jnp.where(mask, e / jnp.sum(e, axis=-1, keepdims=True), 0.0)


def _nsa_prompt_kernel(q_ref, gate_ref, kct_ref, vc_ref, smap_ref, kst_ref, vs_ref, kwt_ref, vw_ref, o_ref,
                       m_scr, l_scr, acc_scr):
    qb = pl.program_id(1)
    tq = q_ref.shape[3]
    n_cmp = kct_ref.shape[3]
    n_sel = smap_ref.shape[1]
    tk = kst_ref.shape[4]
    n_top = min(N_SEL, n_sel)
    scale = HEAD_DIM ** -0.5
    s0 = qb * tq
    pos = s0 + lax.broadcasted_iota(jnp.int32, (tq, 1), 0)
    pos3 = jnp.concatenate([pos] * NSA_GROUP, axis=0)
    gates = gate_ref[0]
    cend = lax.broadcasted_iota(jnp.int32, (1, n_cmp), 1) * CMP_STRIDE + (CMP_LEN - 1)
    blk = lax.broadcasted_iota(jnp.int32, (1, n_sel), 1)
    cur = pos // SEL_BLOCK
    forced = (blk == 0) | (blk == cur) | (blk == cur - 1)
    valid = blk * SEL_BLOCK <= pos
    n_tiles = (s0 + tq + tk - 1) // tk
    n_wt = WINDOW // tq + 1
    wt0 = jnp.maximum(qb - WINDOW // tq, 0)
    kposw = wt0 * tq + lax.broadcasted_iota(jnp.int32, (1, n_wt * tq), 1)
    dist = pos3 - kposw
    mw = (dist >= 0) & (dist <= WINDOW)

    for g in range(NSA_KV_HEADS):
        cols = slice(g * HEAD_DIM, (g + 1) * HEAD_DIM)
        qg = q_ref[0, g].reshape(NSA_GROUP * tq, HEAD_DIM)
        qg_bf = qg.astype(_BF16)
        sc = jnp.dot(qg, kct_ref[0, g], precision=_HI, preferred_element_type=_F32) * scale
        p = _softmax_rows(sc, cend <= pos3)
        oc = jnp.dot(p.astype(_BF16), vc_ref[0, g], preferred_element_type=_F32)
        psum = p[0:tq]
        for r in range(1, NSA_GROUP):
            psum = psum + p[r * tq:(r + 1) * tq]
        imp = jnp.dot(psum, smap_ref[...], precision=_HI, preferred_element_type=_F32)
        imp = jnp.where(valid, imp + jnp.where(forced, FORCE_BONUS, 0.0), NEG)
        _, rank_t = _top_rows(imp.T, n_top)
        sel = (rank_t < float(n_top)).astype(_F32).T.astype(_BF16)
        m_scr[...] = jnp.full(m_scr.shape, NEG, _F32)
        l_scr[...] = jnp.zeros(l_scr.shape, _F32)
        acc_scr[...] = jnp.zeros(acc_scr.shape, _F32)

        def body(kt, carry):
            s = jnp.dot(qg_bf, kst_ref[0, g, kt], preferred_element_type=_F32) * scale
            brow = lax.broadcasted_iota(jnp.int32, (n_sel, tk), 0)
            bkey = kt * (tk // SEL_BLOCK) + lax.broadcasted_iota(jnp.int32, (n_sel, tk), 1) // SEL_BLOCK
            selx = jnp.dot(sel, (brow == bkey).astype(_BF16), preferred_element_type=_F32)
            kpos = kt * tk + lax.broadcasted_iota(jnp.int32, (1, tk), 1)
            mk = jnp.where((selx > 0.5) & (kpos <= pos), 1.0, 0.0)
            mk3 = jnp.concatenate([mk] * NSA_GROUP, axis=0) > 0.5
            sm = jnp.where(mk3, s, NEG)
            m_old = m_scr[...]
            m_new = jnp.maximum(m_old, jnp.max(sm, axis=-1, keepdims=True))
            alpha = jnp.exp(m_old - m_new)
            pe = jnp.where(mk3, jnp.exp(sm - m_new), 0.0)
            l_scr[...] = alpha * l_scr[...] + jnp.sum(pe, axis=-1, keepdims=True)
            acc_scr[...] = alpha * acc_scr[...] + jnp.dot(pe.astype(_BF16), vs_ref[0, kt],
                                                          preferred_element_type=_F32)
            m_scr[...] = m_new
            return carry

        lax.fori_loop(0, n_tiles, body, 0)
        osel = acc_scr[:, cols] / l_scr[...]
        kwt = jnp.concatenate([kwt_ref[0, g, wt0 + i] for i in range(n_wt)], axis=1)
        vw = jnp.concatenate([vw_ref[0, wt0 + i] for i in range(n_wt)], axis=0)
        sw = jnp.dot(qg_bf, kwt, preferred_element_type=_F32) * scale
        pw = _softmax_rows(sw, mw)
        ow = jnp.dot(pw.astype(_BF16), vw, preferred_element_type=_F32)[:, cols]
        for r in range(NSA_GROUP):
            c = (g * NSA_GROUP + r) * 3
            rows = slice(r * tq, (r + 1) * tq)
            o_ref[0, g, r] = (gates[:, c:c + 1] * oc[rows] + gates[:, c + 1:c + 2] * osel[rows]
                              + gates[:, c + 2:c + 3] * ow[rows])


def nsa_prompt_attend(q, kc, vc, sel_kv, win_kv, gates):
    n, s = q.shape[:2]
    G, R, HD = NSA_KV_HEADS, NSA_GROUP, HEAD_DIM
    tq, tk = NSA_QBLOCK, _NSA_SEL_TILE
    n_cmp = kc.shape[1]
    ncp = s // CMP_STRIDE
    n_sel = s // SEL_BLOCK
    assert s % tk == 0 and s >= WINDOW + tq
    padc = ((0, 0), (0, ncp - n_cmp), (0, 0), (0, 0))
    qh = q.transpose(0, 2, 3, 1, 4)
    kct = jnp.pad(kc, padc).transpose(0, 2, 3, 1)
    vcb = jnp.pad(vc, padc).transpose(0, 2, 1, 3).astype(_BF16)
    smap = jnp.pad(selection_map(n_cmp, n_sel), ((0, ncp - n_cmp), (0, 0)))
    kst = sel_kv[:, :, 0].astype(_BF16).reshape(n, s // tk, tk, G, HD).transpose(0, 3, 1, 4, 2)
    vs = sel_kv[:, :, 1].astype(_BF16).reshape(n, s // tk, tk, G * HD)
    kwt = win_kv[:, :, 0].astype(_BF16).reshape(n, s // tq, tq, G, HD).transpose(0, 3, 1, 4, 2)
    vw = win_kv[:, :, 1].astype(_BF16).reshape(n, s // tq, tq, G * HD)
    once = pl.Buffered(1)
    out = pl.pallas_call(
        _nsa_prompt_kernel,
        grid=(n, s // tq),
        in_specs=[
            pl.BlockSpec((1, G, R, tq, HD), lambda b, i: (b, 0, 0, i, 0)),
            pl.BlockSpec((1, tq, G * R * 3), lambda b, i: (b, i, 0)),
            pl.BlockSpec((1, G, HD, ncp), lambda b, i: (b, 0, 0, 0)),
            pl.BlockSpec((1, G, ncp, HD), lambda b, i: (b, 0, 0, 0)),
            pl.BlockSpec((ncp, n_sel), lambda b, i: (0, 0)),
            pl.BlockSpec((1, G, s // tk, HD, tk), lambda b, i: (b, 0, 0, 0, 0), pipeline_mode=once),
            pl.BlockSpec((1, s // tk, tk, G * HD), lambda b, i: (b, 0, 0, 0), pipeline_mode=once),
            pl.BlockSpec((1, G, s // tq, HD, tq), lambda b, i: (b, 0, 0, 0, 0), pipeline_mode=once),
            pl.BlockSpec((1, s // tq, tq, G * HD), lambda b, i: (b, 0, 0, 0), pipeline_mode=once),
        ],
        out_specs=pl.BlockSpec((1, G, R, tq, HD), lambda b, i: (b, 0, 0, i, 0)),
        out_shape=jax.ShapeDtypeStruct((n, G, R, s, HD), _F32),
        scratch_shapes=[pltpu.VMEM((R * tq, 1), _F32), pltpu.VMEM((R * tq, 1), _F32),
                        pltpu.VMEM((R * tq, G * HD), _F32)],
        compiler_params=pltpu.CompilerParams(dimension_semantics=("arbitrary", "arbitrary"),
                                             vmem_limit_bytes=VMEM_BYTES_V7X * 3 // 4),
        name="nsa_prompt",
    )(qh, gates.reshape(n, s, G * R * 3), kct, vcb, smap, kst, vs, kwt, vw)
    return out.transpose(0, 3, 1, 2, 4).reshape(n, s, MIX_W)


_BF16_ROWS = 16


def _softmax_cols_aug(sT, mask, v_aug):
    sm = jnp.where(mask, sT, NEG)
    e = jnp.where(mask, jnp.exp(sm - jnp.max(sm, axis=0, keepdims=True)), 0.0)
    acc = lax.dot_general(e.astype(_BF16), v_aug, (((0,), (0,)), ((), ())), preferred_element_type=_F32)
    c = v_aug.shape[1] - LANES
    return acc[:, :c] / acc[:, c:c + 1]


def _nsa_sample_kernel(pt_ref, qbd_ref, kc_ref, vc_ref, smapt_ref, rsum_ref, rexp_ref, gate_ref, *rest,
                       n_pages, past, t_new, n_cmp, n_sel, wb):
    page_refs = rest[:n_pages]
    selnew_ref, win_ref, o_ref, s_scr = rest[n_pages:]
    kw = NSA_KV_HEADS * HEAD_DIM
    scale = HEAD_DIM ** -0.5
    n_top = min(N_SEL, n_sel)
    qbd = qbd_ref[0]
    qbd_bf = qbd.astype(_BF16)
    nl = qbd.shape[1]
    pos = past + lax.broadcasted_iota(jnp.int32, (1, nl), 1) % t_new
    ones_cols = lambda k: jnp.ones((k, LANES), _BF16)

    ncp = kc_ref.shape[1]
    ci = lax.broadcasted_iota(jnp.int32, (ncp, 1), 0)
    mc = (ci * CMP_STRIDE + (CMP_LEN - 1) <= pos) & (ci < n_cmp)
    sc = jnp.dot(kc_ref[0], qbd, precision=_HI, preferred_element_type=_F32) * scale
    scm = jnp.where(mc, sc, NEG)
    e = jnp.exp(scm - jnp.max(scm, axis=0, keepdims=True))
    pc = jnp.where(mc, e / jnp.sum(e, axis=0, keepdims=True), 0.0)
    oc = lax.dot_general(pc.astype(_BF16), vc_ref[0], (((0,), (0,)), ((), ())), preferred_element_type=_F32)
    psum = jnp.dot(pc, rsum_ref[...], precision=_HI, preferred_element_type=_F32)
    imp = jnp.dot(smapt_ref[...], psum, precision=_HI, preferred_element_type=_F32)
    nb, nq = imp.shape
    pos_q = past + lax.broadcasted_iota(jnp.int32, (1, nq), 1) % t_new
    blk = lax.broadcasted_iota(jnp.int32, (nb, 1), 0)
    cur = pos_q // SEL_BLOCK
    forced = (blk == 0) | (blk == cur) | (blk == cur - 1)
    imp = jnp.where(blk * SEL_BLOCK <= pos_q, imp + jnp.where(forced, FORCE_BONUS, 0.0), NEG)
    imp = jnp.where(blk < n_sel, imp, -jnp.inf)
    _, rank = _top_rows(imp, n_top)
    sel = jnp.dot((rank < float(n_top)).astype(_BF16), rexp_ref[...], preferred_element_type=_F32)

    rowi = lax.broadcasted_iota(jnp.int32, (PAGE_SIZE, 1), 0)
    per_page = PAGE_SIZE // SEL_BLOCK
    for p in range(n_pages):
        kp = page_refs[p][0, :, 0:kw].astype(_BF16)
        s = jnp.dot(kp, qbd_bf, preferred_element_type=_F32) * scale
        pick = sel[p * per_page:p * per_page + 1, :]
        for b in range(1, per_page):
            pick = jnp.where(rowi >= b * SEL_BLOCK, sel[p * per_page + b:p * per_page + b + 1, :], pick)
        ok = (pick > 0.5) & (p * PAGE_SIZE + rowi <= pos)
        s_scr[p * PAGE_SIZE:(p + 1) * PAGE_SIZE, :] = jnp.where(ok, s, NEG)
    n_new = selnew_ref.shape[1]
    rown = lax.broadcasted_iota(jnp.int32, (n_new, 1), 0)
    kn = selnew_ref[0, :, 0:kw].astype(_BF16).astype(_F32)
    s = jnp.dot(kn, qbd_bf.astype(_F32), precision=_HI, preferred_element_type=_F32) * scale
    ok = (sel[past // SEL_BLOCK:past // SEL_BLOCK + 1, :] > 0.5) & (past + rown <= pos)
    s_scr[past:past + n_new, :] = jnp.where(ok, s, NEG)
    mx = jnp.max(s_scr[...], axis=0, keepdims=True)
    acc = jnp.zeros((nl, kw + LANES), _F32)
    for p in range(n_pages):
        sp = s_scr[p * PAGE_SIZE:(p + 1) * PAGE_SIZE, :]
        ep = jnp.where(sp > 0.5 * NEG, jnp.exp(sp - mx), 0.0).astype(_BF16)
        v_aug = jnp.concatenate([page_refs[p][0, :, kw:2 * kw].astype(_BF16), ones_cols(PAGE_SIZE)], axis=1)
        acc = acc + lax.dot_general(ep, v_aug, (((0,), (0,)), ((), ())), preferred_element_type=_F32)
    sp = s_scr[past:past + n_new, :]
    ep = jnp.where(sp > 0.5 * NEG, jnp.exp(sp - mx), 0.0).astype(_BF16)
    v_aug = jnp.concatenate([selnew_ref[0, :, kw:2 * kw].astype(_BF16), ones_cols(n_new)], axis=1)
    acc = acc + lax.dot_general(ep, v_aug, (((0,), (0,)), ((), ())), preferred_element_type=_F32)
    osel = acc[:, :kw] / acc[:, kw:kw + 1]

    n_w = win_ref.shape[1]
    wpos = past - wb + lax.broadcasted_iota(jnp.int32, (n_w, 1), 0)
    dist = pos - wpos
    mw = (dist >= 0) & (dist <= WINDOW) & (wpos >= 0)
    sw = jnp.dot(win_ref[0, :, 0:kw].astype(_BF16), qbd_bf, preferred_element_type=_F32) * scale
    v_aug = jnp.concatenate([win_ref[0, :, kw:2 * kw].astype(_BF16), ones_cols(n_w)], axis=1)
    ow = _softmax_cols_aug(sw, mw, v_aug)

    g = gate_ref[0]
    o_ref[0] = g[:, 0:1] * oc + g[:, 1:2] * osel + g[:, 2:3] * ow


def nsa_sample_attend(q, kc, vc, pool_sel, page_table, sel_new, win_all, gates, wb):
    n, t_new = q.shape[:2]
    G, R, HD = NSA_KV_HEADS, NSA_GROUP, HEAD_DIM
    n_pages = page_table.shape[1]
    past = n_pages * PAGE_SIZE
    n_cmp = kc.shape[1]
    n_sel = -(-(past + t_new) // SEL_BLOCK)
    assert past % SEL_BLOCK == 0 and t_new <= SEL_BLOCK and PAGE_SIZE % SEL_BLOCK == 0
    kw = G * HD
    nl = G * R * t_new
    up = lambda v, m: -(-v // m) * m
    ncp, nbp = up(n_cmp, LANES), up(n_sel, 8)
    n_new, n_w = up(t_new, _BF16_ROWS), up(wb + t_new, _BF16_ROWS)
    qbd = jnp.einsum("ntgrd,gh->ngdhrt", q, jnp.eye(G, dtype=q.dtype)).reshape(n, kw, nl)
    kcp = jnp.pad(kc.reshape(n, n_cmp, kw), ((0, 0), (0, ncp - n_cmp), (0, 0)))
    vcp = jnp.pad(vc.reshape(n, n_cmp, kw), ((0, 0), (0, ncp - n_cmp), (0, 0))).astype(_BF16)
    smapt = jnp.pad(selection_map(n_cmp, n_sel).T, ((0, nbp - n_sel), (0, ncp - n_cmp)))
    rsum = np.zeros((G, R, t_new, G, t_new), np.float32)
    for g in range(G):
        for t in range(t_new):
            rsum[g, :, t, g, t] = 1.0
    rsum = rsum.reshape(nl, G * t_new)
    gates3 = gates.transpose(0, 2, 3, 1, 4).reshape(n, nl, 3)
    pages = pool_sel.reshape(pool_sel.shape[0], PAGE_SIZE, 2 * kw)
    seln = jnp.pad(sel_new.reshape(n, t_new, 2 * kw), ((0, 0), (0, n_new - t_new), (0, 0)))
    win = jnp.pad(win_all.reshape(n, wb + t_new, 2 * kw), ((0, 0), (0, n_w - wb - t_new), (0, 0)))
    whole = lambda *shape: pl.BlockSpec(shape, lambda b, pt: (0,) * len(shape))
    per_seq = lambda *shape: pl.BlockSpec((1,) + shape, lambda b, pt: (b,) + (0,) * len(shape))
    page_spec = lambda p: pl.BlockSpec((1, PAGE_SIZE, 2 * kw), lambda b, pt: (pt[b, p], 0, 0))
    out = pl.pallas_call(
        functools.partial(_nsa_sample_kernel, n_pages=n_pages, past=past, t_new=t_new, n_cmp=n_cmp, n_sel=n_sel,
                          wb=wb),
        grid_spec=pltpu.PrefetchScalarGridSpec(
            num_scalar_prefetch=1,
            grid=(n,),
            in_specs=[per_seq(kw, nl), per_seq(ncp, kw), per_seq(ncp, kw), whole(nbp, ncp),
                      whole(nl, G * t_new), whole(G * t_new, nl), per_seq(nl, 3)]
                     + [page_spec(p) for p in range(n_pages)]
                     + [per_seq(n_new, 2 * kw), per_seq(n_w, 2 * kw)],
            out_specs=per_seq(nl, kw),
            scratch_shapes=[pltpu.VMEM((past + n_new, nl), _F32)],
        ),
        out_shape=jax.ShapeDtypeStruct((n, nl, kw), _F32),
        compiler_params=pltpu.CompilerParams(dimension_semantics=("arbitrary",),
                                             vmem_limit_bytes=VMEM_BYTES_V7X // 2),
        name="nsa_sample",
    )(page_table, qbd, kcp, vcp, smapt, jnp.asarray(rsum), jnp.asarray(rsum.T).astype(_BF16), gates3,
      *([pages] * n_pages), seln, win)
    o = out.reshape(n, G, R, t_new, G, HD)
    o = jnp.stack([o[:, g, :, :, g, :] for g in range(G)], axis=1)
    return o.transpose(0, 3, 1, 2, 4).reshape(n, t_new, MIX_W)


def nsa_sample(z, pool_cmp, pool_sel, win_buf, page_table, q_g, k_g, pe, w1, b1, w2):
    n, t_new, _ = z.shape
    past = page_table.shape[1] * PAGE_SIZE
    G, HD = NSA_KV_HEADS, HEAD_DIM
    q, cmp_new, sel_new, win_new, gates = nsa_split(z, q_g, k_g)
    win_all = jnp.concatenate([win_buf, win_new], axis=1)
    wb = win_buf.shape[1]
    n_ch = (past + t_new) // CMP_STRIDE
    assert PAGE_SIZE % CMP_STRIDE == 0 and n_ch * CMP_STRIDE <= past
    per_page = PAGE_SIZE // CMP_STRIDE
    pool6 = pool_cmp.reshape(pool_cmp.shape[0], per_page, CMP_STRIDE, 2, G, HD)
    w1r = w1.reshape(2, 2, CMP_STRIDE, HD, CMP_HIDDEN)
    a_pool = jnp.einsum("pcskgd,kjsdh->pckjgh", pool6, w1r)
    a_seq = a_pool[page_table].reshape(n, past // CMP_STRIDE, 2, 2, G, CMP_HIDDEN)[:, :n_ch]
    bias = jnp.einsum("kx,kxh->kh", pe.reshape(2, -1), w1) + b1
    pre = a_seq[:, :-1, :, 0] + a_seq[:, 1:, :, 1] + bias[None, None, :, None, :]
    comp = jnp.einsum("nckgh,khd->nckgd", jax.nn.gelu(pre, approximate=False), w2)
    kc = rms_norm(comp[:, :, 0], k_g[0])
    vc = comp[:, :, 1]
    out = nsa_sample_attend(q, kc, vc, pool_sel, page_table, sel_new, win_all, gates, wb)
    return out, cmp_new, sel_new, win_all[:, -wb:]


def shortconv_mix(z, ctx, w, b):
    bg = z[..., :MIX_W]
    cg = z[..., MIX_W:2 * MIX_W]
    v = z[..., 2 * MIX_W:3 * MIX_W]
    u_ext = jnp.concatenate([ctx, cg * v], axis=1)
    y = bg * causal_dwconv(u_ext, w, b)
    return y, u_ext[:, -(CONV_B_WIDTH - 1):]


def conformer_mix(z, ctx, w, b, ln_g, ln_b):
    u = z[..., :MIX_W] * jax.nn.sigmoid(z[..., MIX_W:2 * MIX_W])
    u_ext = jnp.concatenate([ctx, u], axis=1)
    y = jax.nn.silu(layer_norm(causal_dwconv(u_ext, w, b), ln_g, ln_b))
    return y, u_ext[:, -(CONV_C_WIDTH - 1):]


_CAND = [(j1, j2) for j1 in range(PEER_TOPK) for j2 in range(PEER_TOPK) if (j1 + 1) * (j2 + 1) <= PEER_TOPK]
_N_CAND = len(_CAND)
_CAND_ROWS = -(-_N_CAND // 8) * 8
_PEER_SEL_TILE = 256
_PEER_TOK_TILE = 512
_PEER_EXP_BLOCK = 1024
_PEER_SUB_BLOCK = 256
_PEER_ROW_GROUP = 32


def _cand_tables():
    j1 = np.full((_CAND_ROWS, LANES), -1.0, np.float32)
    j2 = np.full((_CAND_ROWS, LANES), -1.0, np.float32)
    flat = np.full((_CAND_ROWS, LANES), 1.0e6, np.float32)
    for r, (a, b) in enumerate(_CAND):
        j1[r], j2[r], flat[r] = a, b, a * PEER_TOPK + b
    return np.stack([j1, j2, flat])


def _gelu_erf(x):
    return 0.5 * x * (1.0 + lax.erf(x * (2.0 ** -0.5)))


def _top_rows(s, n_top):
    (vals, rank), = _top_rows_many([s], n_top)
    return vals, rank


def _top_rows_many(arrays, n_top):
    k, l = arrays[0].shape
    kidx = lax.broadcasted_iota(jnp.int32, (k, l), 0).astype(_F32)
    jrow = lax.broadcasted_iota(jnp.int32, (n_top, l), 0)
    work = list(arrays)
    rank = [jnp.full((k, l), float(n_top), _F32) for _ in arrays]
    vals = [jnp.zeros((n_top, l), _F32) for _ in arrays]
    for j in range(n_top):
        for i in range(len(arrays)):
            m = jnp.max(work[i], axis=0, keepdims=True)
            first = jnp.min(jnp.where(work[i] == m, kidx, float(k)), axis=0, keepdims=True)
            sel = kidx == first
            rank[i] = jnp.where(sel, float(j), rank[i])
            work[i] = jnp.where(sel, -jnp.inf, work[i])
            vals[i] = jnp.where(jrow == j, m, vals[i])
    return list(zip(vals, rank))


def _peer_select_kernel(x_ref, g_ref, wq_ref, sk_ref, ct_ref, ht_ref, p1_ref, c1_ref, p2_ref, r2_ref, h_scr):
    hd = pl.program_id(1)

    @pl.when(hd == 0)
    def _():
        x = x_ref[...]
        h = x * lax.rsqrt(jnp.mean(x * x, axis=-1, keepdims=True) + EPS) * g_ref[...]
        h_scr[...] = h
        ht_ref[...] = h.T.astype(_BF16)

    q = jnp.dot(h_scr[...], wq_ref[...], precision=_HI, preferred_element_type=_F32)
    nt = (((1,), (1,)), ((), ()))
    half = PEER_DK // 2
    s1_all = lax.dot_general(sk_ref[0, 0], q[:, :half], nt, precision=_HI, preferred_element_type=_F32)
    s2_all = lax.dot_general(sk_ref[0, 1], q[:, half:], nt, precision=_HI, preferred_element_type=_F32)
    cj1, cj2, cflat = ct_ref[0], ct_ref[1], ct_ref[2]
    for c in range(x_ref.shape[0] // LANES):
        sl = slice(c * LANES, (c + 1) * LANES)
        s1, s2 = s1_all[:, sl], s2_all[:, sl]
        (v1, rank1), (v2, rank2) = _top_rows_many([s1, s2], PEER_TOPK)
        a1 = jnp.zeros((_CAND_ROWS, LANES), _F32)
        a2 = jnp.zeros((_CAND_ROWS, LANES), _F32)
        for j in range(PEER_TOPK):
            a1 = jnp.where(cj1 == float(j), v1[j:j + 1, :], a1)
            a2 = jnp.where(cj2 == float(j), v2[j:j + 1, :], a2)
        cand = jnp.where(cj1 >= 0.0, a1 + a2, -jnp.inf)
        work = cand
        picked = jnp.zeros((_CAND_ROWS, LANES), _F32)
        for _ in range(PEER_TOPK):
            m = jnp.max(work, axis=0, keepdims=True)
            first = jnp.min(jnp.where(work == m, cflat, 2.0e6), axis=0, keepdims=True)
            sel = cflat == first
            picked = jnp.where(sel, 1.0, picked)
            work = jnp.where(sel, -jnp.inf, work)
        top = v1[0:1, :] + v2[0:1, :]
        z = jnp.sum(jnp.where(picked > 0.0, jnp.exp(cand - top), 0.0), axis=0, keepdims=True)
        c1 = jnp.zeros((PEER_KEYS, LANES), _F32)
        for j in range(PEER_TOPK):
            cnt_j = jnp.sum(jnp.where(cj1 == float(j), picked, 0.0), axis=0, keepdims=True)
            c1 = jnp.where(rank1 == float(j), cnt_j, c1)
        p1_ref[0, :, sl] = jnp.exp(s1 - v1[0:1, :]) / z
        c1_ref[0, :, sl] = c1
        p2_ref[0, :, sl] = jnp.exp(s2 - v2[0:1, :])
        r2_ref[0, :, sl] = rank2


def _peer_select(x, g, w_q, sub_keys):
    n, d = x.shape
    ts = _PEER_SEL_TILE
    assert n % ts == 0
    fac = jax.ShapeDtypeStruct((PEER_HEADS, PEER_KEYS, n), _F32)
    fac_spec = pl.BlockSpec((1, PEER_KEYS, ts), lambda i, h: (h, 0, i))
    return pl.pallas_call(
        _peer_select_kernel,
        grid=(n // ts, PEER_HEADS),
        in_specs=[
            pl.BlockSpec((ts, d), lambda i, h: (i, 0)),
            pl.BlockSpec((1, d), lambda i, h: (0, 0)),
            pl.BlockSpec((d, PEER_DK), lambda i, h: (0, h)),
            pl.BlockSpec((1, 2, PEER_KEYS, PEER_DK // 2), lambda i, h: (h, 0, 0, 0)),
            pl.BlockSpec((3, _CAND_ROWS, LANES), lambda i, h: (0, 0, 0)),
        ],
        out_specs=[pl.BlockSpec((d, ts), lambda i, h: (0, i)), fac_spec, fac_spec, fac_spec, fac_spec],
        out_shape=[jax.ShapeDtypeStruct((d, n), _BF16), fac, fac, fac, fac],
        scratch_shapes=[pltpu.VMEM((ts, d), _F32)],
        compiler_params=pltpu.CompilerParams(dimension_semantics=("arbitrary", "arbitrary"),
                                             vmem_limit_bytes=VMEM_BYTES_V7X * 3 // 4),
        name="peer_select",
    )(x, g.reshape(1, d), w_q, sub_keys, jnp.asarray(_cand_tables()))


def _peer_dense_kernel(x_ref, ht_ref, u_ref, vt_ref, p1_ref, c1_ref, p2_ref, r2_ref, o_ref,
                       acc_ref, *scratch):
    j = pl.program_id(1)

    @pl.when(j == 0)
    def _():
        acc_ref[...] = jnp.zeros_like(acc_ref)

    eb, tt = u_ref.shape[0], ht_ref.shape[1]
    sub = _PEER_SUB_BLOCK
    n_sub = eb // sub
    st_refs, wact_refs = scratch[:2], scratch[2:]
    half = tt // 2
    d = vt_ref.shape[0]
    n_out = d // sub

    def scores(k, nh):
        lanes = slice(nh * half, (nh + 1) * half)
        st_refs[k % 2][:, lanes] = jnp.dot(u_ref[k * sub:(k + 1) * sub, :], ht_ref[:, lanes],
                                           preferred_element_type=_F32)

    def output_piece(k, q):
        r = slice(q * sub, (q + 1) * sub)
        acc_ref[r, :] += jnp.dot(vt_ref[r, k * sub:(k + 1) * sub], wact_refs[k % 2][...],
                                 preferred_element_type=_F32)

    def build_chunk(k, al, c, a):
        lanes = slice(c * LANES, (c + 1) * LANES)
        c1 = [c1_ref[h, pl.ds(a, 1), :][:, lanes] for h in range(PEER_HEADS)]
        p1 = [p1_ref[h, pl.ds(a, 1), :][:, lanes] for h in range(PEER_HEADS)]
        for rg in range(PEER_KEYS // _PEER_ROW_GROUP):
            b = slice(rg * _PEER_ROW_GROUP, (rg + 1) * _PEER_ROW_GROUP)
            brows = slice(al * PEER_KEYS + b.start, al * PEER_KEYS + b.stop)
            w = jnp.zeros((_PEER_ROW_GROUP, LANES), _F32)
            for h in range(PEER_HEADS):
                w = w + jnp.where(r2_ref[h, b, lanes] < c1[h], p2_ref[h, b, lanes], 0.0) * p1[h]
            wact_refs[k % 2][brows, lanes] = (w * _gelu_erf(st_refs[k % 2][brows, lanes])).astype(_BF16)

    n_al, n_c = sub // PEER_KEYS, tt // LANES
    n_chunks = n_al * n_c
    scores(0, 0)
    scores(0, 1)
    for k in range(n_sub):
        i = 0
        for al in range(n_al):
            a = j * (eb // PEER_KEYS) + k * n_al + al
            for c in range(n_c):
                build_chunk(k, al, c, a)
                if k + 1 < n_sub and i % (n_chunks // 2) == 0:
                    scores(k + 1, i // (n_chunks // 2))
                if k >= 1 and i % (n_chunks // n_out) == n_chunks // n_out - 1:
                    output_piece(k - 1, i // (n_chunks // n_out))
                i += 1
    for q in range(n_out):
        output_piece(n_sub - 1, q)

    @pl.when(j == pl.num_programs(1) - 1)
    def _():
        o_ref[...] = x_ref[...] + acc_ref[...].T


def _peer_dense(x, ht, u_bf, vt_bf, p1, c1, p2, r2):
    n, d = x.shape
    tt, eb = _PEER_TOK_TILE, _PEER_EXP_BLOCK
    assert n % tt == 0 and N_EXPERTS % eb == 0
    fac_spec = pl.BlockSpec((PEER_HEADS, PEER_KEYS, tt), lambda i, j: (0, 0, i))
    return pl.pallas_call(
        _peer_dense_kernel,
        grid=(n // tt, N_EXPERTS // eb),
        in_specs=[
            pl.BlockSpec((tt, d), lambda i, j: (i, 0)),
            pl.BlockSpec((d, tt), lambda i, j: (0, i)),
            pl.BlockSpec((eb, d), lambda i, j: (j, 0)),
            pl.BlockSpec((d, eb), lambda i, j: (0, j)),
            fac_spec, fac_spec, fac_spec, fac_spec,
        ],
        out_specs=pl.BlockSpec((tt, d), lambda i, j: (i, 0)),
        out_shape=jax.ShapeDtypeStruct((n, d), _F32),
        scratch_shapes=([pltpu.VMEM((d, tt), _F32)]
                        + [pltpu.VMEM((_PEER_SUB_BLOCK, tt), _F32)] * 2
                        + [pltpu.VMEM((_PEER_SUB_BLOCK, tt), _BF16)] * 2),
        compiler_params=pltpu.CompilerParams(dimension_semantics=("arbitrary", "arbitrary"),
                                             vmem_limit_bytes=VMEM_BYTES_V7X * 3 // 4),
        name="peer_dense",
    )(x, ht, u_bf, vt_bf, p1, c1, p2, r2)


def peer_residual(x, g, w_q, sub_keys, u_tab, v_tab):
    ht, p1, c1, p2, r2 = _peer_select(x, g, w_q, sub_keys)
    return _peer_dense(x, ht, u_tab.astype(_BF16), v_tab.T.astype(_BF16), p1, c1, p2, r2)


def kernel(x_prompt, x_sample, cache_a_cmp_kv, cache_a_sel_kv, cache_a_win_kv, state_b_conv, state_c_conv,
           cache_mem_kv, page_table, mem_prompt, norm_mix_g, norm_mem_g, w_mem_kv, mem_q_norm_g, mem_k_norm_g,
           w_out, norm_ffn_g, peer_w_q, peer_sub_keys, peer_u, peer_v, a_w_in, a_q_norm_g, a_k_norm_g,
           a_cmp_pe, a_cmp_w1, a_cmp_b1, a_cmp_w2, b_w_in, b_conv_w, b_conv_b, c_w_in, c_conv_w, c_conv_b,
           c_ln_g, c_ln_b):
    xp, xs = x_prompt, x_sample
    n_p, n_s = xp.shape[0], xs.shape[0]
    n_tok_p = n_p * xp.shape[1]
    p_cmp, p_sel, p_win, p_cb, p_cc, p_mem = [], [], [], [], [], []
    s_cmp, s_sel, s_win, s_cb, s_cc = [], [], [], [], []
    for i in range(DEPTH):
        kind, li = i % N_MIXERS, i // N_MIXERS
        hp = rms_norm(xp, norm_mix_g[i])
        hs = rms_norm(xs, norm_mix_g[i])
        if kind == 0:
            zp = hp @ a_w_in[li]
            zs = hs @ a_w_in[li]
            nsa_w = (a_q_norm_g[li], a_k_norm_g[li], a_cmp_pe[li], a_cmp_w1[li], a_cmp_b1[li], a_cmp_w2[li])
            mp, c_p, sl_p, w_p = nsa_prompt(zp, *nsa_w)
            ms, c_s, sl_s, w_s = nsa_sample(zs, cache_a_cmp_kv[li], cache_a_sel_kv[li], cache_a_win_kv[li],
                                            page_table, *nsa_w)
            p_cmp.append(c_p)
            p_sel.append(sl_p)
            p_win.append(w_p)
            s_cmp.append(c_s)
            s_sel.append(sl_s)
            s_win.append(w_s)
        elif kind == 1:
            zp = hp @ b_w_in[li]
            zs = hs @ b_w_in[li]
            mp, st_p = shortconv_mix(zp, jnp.zeros((n_p, CONV_B_WIDTH - 1, MIX_W), zp.dtype), b_conv_w[li], b_conv_b[li])
            ms, st_s = shortconv_mix(zs, state_b_conv[li], b_conv_w[li], b_conv_b[li])
            p_cb.append(st_p)
            s_cb.append(st_s)
        else:
            zp = hp @ c_w_in[li]
            zs = hs @ c_w_in[li]
            mp, st_p = conformer_mix(zp, jnp.zeros((n_p, CONV_C_WIDTH - 1, MIX_W), zp.dtype),
                                     c_conv_w[li], c_conv_b[li], c_ln_g[li], c_ln_b[li])
            ms, st_s = conformer_mix(zs, state_c_conv[li], c_conv_w[li], c_conv_b[li], c_ln_g[li], c_ln_b[li])
            p_cc.append(st_p)
            s_cc.append(st_s)
        mkv_p = memory_kv(mem_prompt, norm_mem_g[i], w_mem_kv[i], mem_k_norm_g[i])
        p_mem.append(mkv_p)
        op = jnp.concatenate([mp, memory_attend(zp[..., -MEM_W:], mkv_p, mem_q_norm_g[i])], axis=-1)
        os_ = jnp.concatenate([ms, memory_attend(zs[..., -MEM_W:], cache_mem_kv[i], mem_q_norm_g[i])], axis=-1)
        xp = xp + op @ w_out[i]
        xs = xs + os_ @ w_out[i]
        x_all = jnp.concatenate([xp.reshape(-1, D_MODEL), xs.reshape(-1, D_MODEL)], axis=0)
        x_all = peer_residual(x_all, norm_ffn_g[i], peer_w_q[i], peer_sub_keys[i], peer_u[i], peer_v[i])
        xp = x_all[:n_tok_p].reshape(xp.shape)
        xs = x_all[n_tok_p:].reshape(xs.shape)
    return (xp, xs, jnp.stack(p_cmp), jnp.stack(p_sel), jnp.stack(p_win), jnp.stack(p_cb), jnp.stack(p_cc),
            jnp.stack(p_mem), jnp.stack(s_cmp), jnp.stack(s_sel), jnp.stack(s_win), jnp.stack(s_cb), jnp.stack(s_cc))
```

```python
import functools

import numpy as np
import jax
import jax.numpy as jnp
from jax import lax
from jax.experimental import pallas as pl
from jax.experimental.pallas import tpu as pltpu

D_MODEL = 1024
BATCH = 2
SEQ = 8192
DEPTH = 4
DEC_BATCH = 128
DEC_SEQ = 8
PAST_LEN = 2048
PAGE_SIZE = 128

N_MIXERS = 3
HEAD_DIM = 64
MIX_W = 3 * D_MODEL // 4
MEM_W = D_MODEL - MIX_W
MEM_HEADS = MEM_W // HEAD_DIM
N_MEM = 256
NSA_HEADS = MIX_W // HEAD_DIM
NSA_KV_HEADS = 4
NSA_GROUP = NSA_HEADS // NSA_KV_HEADS
NSA_KV_W = 2 * NSA_KV_HEADS * HEAD_DIM
CMP_STRIDE = 16
CMP_LEN = 2 * CMP_STRIDE
CMP_HIDDEN = 64
SEL_BLOCK = 64
N_SEL = 16
WINDOW = 512
NSA_QBLOCK = 128
FORCE_BONUS = 1.0e4
A_IN_W = MIX_W + 3 * NSA_KV_W + 3 * NSA_HEADS + MEM_W
CONV_B_WIDTH = 3
B_IN_W = 3 * MIX_W + MEM_W
CONV_C_WIDTH = 31
C_IN_W = 2 * MIX_W + MEM_W
PEER_HEADS = 8
PEER_KEYS = 128
N_EXPERTS = PEER_KEYS * PEER_KEYS
PEER_TOPK = 16
PEER_DK = 256
EPS = 1e-6
NEG = -1e30

LANES = 128
VMEM_BYTES_V7X = 64 * 1024 * 1024

_F32 = jnp.float32
_BF16 = jnp.bfloat16
_HI = lax.Precision.HIGHEST


def rms_norm(x, g):
    xf = x.astype(jnp.float32)
    y = xf * lax.rsqrt(jnp.mean(xf * xf, axis=-1, keepdims=True) + EPS)
    return (y * g.astype(jnp.float32)).astype(x.dtype)


def layer_norm(x, g, b):
    xf = x.astype(jnp.float32)
    mu = jnp.mean(xf, axis=-1, keepdims=True)
    var = jnp.mean(jnp.square(xf - mu), axis=-1, keepdims=True)
    y = (xf - mu) * lax.rsqrt(var + EPS)
    return (y * g.astype(jnp.float32) + b.astype(jnp.float32)).astype(x.dtype)


def masked_softmax(s, mask):
    p = jax.nn.softmax(jnp.where(mask, s.astype(jnp.float32), NEG), axis=-1)
    return jnp.where(mask, p, 0.0)


def causal_dwconv(u_ext, w, b):
    y = lax.conv_general_dilated(u_ext, w[:, None, :].astype(u_ext.dtype), window_strides=(1,),
                                 padding="VALID", dimension_numbers=("NWC", "WIO", "NWC"),
                                 feature_group_count=u_ext.shape[-1])
    return y + b


_PROJ_ROWS = 512
_PROJ_MAX_COLS = 1024


def _norm_proj_kernel(x_ref, g_ref, w_ref, o_ref, h_scr):
    @pl.when(pl.program_id(1) == 0)
    def _():
        x = x_ref[...]
        h = x * lax.rsqrt(jnp.mean(x * x, axis=-1, keepdims=True) + EPS) * g_ref[...]
        h_scr[...] = h.astype(_BF16)

    o_ref[...] = jnp.dot(h_scr[...], w_ref[...], preferred_element_type=_F32)


def norm_proj(x, g, w):
    n, d = x.shape
    m = w.shape[1]
    groups = -(-m // LANES)
    per_tile = max(k for k in range(1, _PROJ_MAX_COLS // LANES + 1) if groups % k == 0)
    tn, mp = per_tile * LANES, groups * LANES
    rows = min(_PROJ_ROWS, n)
    assert n % rows == 0
    wp = jnp.pad(w, ((0, 0), (0, mp - m))).astype(_BF16)
    return pl.pallas_call(
        _norm_proj_kernel,
        grid=(n // rows, mp // tn),
        in_specs=[pl.BlockSpec((rows, d), lambda i, j: (i, 0)),
                  pl.BlockSpec((1, d), lambda i, j: (0, 0)),
                  pl.BlockSpec((d, tn), lambda i, j: (0, j))],
        out_specs=pl.BlockSpec((rows, tn), lambda i, j: (i, j)),
        out_shape=jax.ShapeDtypeStruct((n, mp), _F32),
        scratch_shapes=[pltpu.VMEM((rows, d), _BF16)],
        compiler_params=pltpu.CompilerParams(dimension_semantics=("arbitrary", "arbitrary")),
        name="norm_proj",
    )(x, g.reshape(1, d), wp)


def _out_proj_kernel(o_ref, w_ref, x_ref, y_ref):
    y_ref[...] = x_ref[...] + jnp.dot(o_ref[...].astype(_BF16), w_ref[...], preferred_element_type=_F32)


def out_proj_residual(o, w, x):
    n, d = x.shape
    rows = min(_PROJ_ROWS, n)
    assert n % rows == 0 and w.shape == (d, d)
    row_spec = pl.BlockSpec((rows, d), lambda i: (i, 0))
    return pl.pallas_call(
        _out_proj_kernel,
        grid=(n // rows,),
        in_specs=[row_spec, pl.BlockSpec((d, d), lambda i: (0, 0)), row_spec],
        out_specs=row_spec,
        out_shape=jax.ShapeDtypeStruct((n, d), _F32),
        compiler_params=pltpu.CompilerParams(dimension_semantics=("arbitrary",)),
        name="out_proj",
    )(o, w.astype(_BF16), x)


def memory_kv(mem, g_norm, w_kv, k_g):
    n, m, _ = mem.shape
    kv = norm_proj(mem.reshape(n * m, -1), g_norm, w_kv)[:, :w_kv.shape[1]].reshape(n, m, 2, MEM_HEADS, HEAD_DIM)
    return jnp.stack([rms_norm(kv[:, :, 0], k_g), kv[:, :, 1]], axis=2)


def memory_attend(zq, mkv, q_g):
    n, t, _ = zq.shape
    q = rms_norm(zq.reshape(n, t, MEM_HEADS, HEAD_DIM), q_g)
    s = jnp.einsum("nthd,nmhd->nhtm", q, mkv[:, :, 0]) * (HEAD_DIM ** -0.5)
    p = jax.nn.softmax(s.astype(jnp.float32), axis=-1).astype(zq.dtype)
    return jnp.einsum("nhtm,nmhd->nthd", p, mkv[:, :, 1]).reshape(n, t, MEM_W)


def nsa_split(z, q_g, k_g):
    n, t, _ = z.shape
    G, R, HD = NSA_KV_HEADS, NSA_GROUP, HEAD_DIM
    q = rms_norm(z[..., :MIX_W].reshape(n, t, G, R, HD), q_g)
    o = MIX_W
    cmp_kv = z[..., o:o + NSA_KV_W].reshape(n, t, 2, G, HD)
    sel = z[..., o + NSA_KV_W:o + 2 * NSA_KV_W].reshape(n, t, 2, G, HD)
    win = z[..., o + 2 * NSA_KV_W:o + 3 * NSA_KV_W].reshape(n, t, 2, G, HD)
    sel_kv = jnp.stack([rms_norm(sel[:, :, 0], k_g[1]), sel[:, :, 1]], axis=2)
    win_kv = jnp.stack([rms_norm(win[:, :, 0], k_g[2]), win[:, :, 1]], axis=2)
    o = o + 3 * NSA_KV_W
    gates = jax.nn.sigmoid(z[..., o:o + 3 * NSA_HEADS]).reshape(n, t, G, R, 3)
    return q, cmp_kv, sel_kv, win_kv, gates


def compress(rows, pe, w1, b1, w2):
    n, length, G, HD = rows.shape
    n_ch = length // CMP_STRIDE
    ch = rows[:, :n_ch * CMP_STRIDE].reshape(n, n_ch, CMP_STRIDE, G, HD)
    a = jnp.einsum("ncsgd,ksdh->nckgh", ch, w1.reshape(2, CMP_STRIDE, HD, CMP_HIDDEN))
    bias = pe.reshape(-1) @ w1 + b1
    pre = a[:, :-1, 0] + a[:, 1:, 1] + bias
    return jax.nn.gelu(pre, approximate=False) @ w2


def selection_map(n_cmp, n_sel):
    start = np.arange(n_cmp)[:, None] * CMP_STRIDE
    blk = np.arange(n_sel)[None, :] * SEL_BLOCK
    m = (start <= blk + SEL_BLOCK - 1) & (start + CMP_LEN - 1 >= blk)
    return jnp.asarray(m.astype(np.float32))


def nsa_block(q, q_pos, kc, vc, c_end, ks, vs, kw, vw, w_pos, gates, sel_map):
    scale = HEAD_DIM ** -0.5
    tq = q.shape[0]
    n_sel = sel_map.shape[1]
    m_c = (c_end[None, :] <= q_pos[:, None])[:, None, None, :]
    p_c = masked_softmax(jnp.einsum("tgrd,ngd->tgrn", q, kc) * scale, m_c)
    o_c = jnp.einsum("tgrn,ngd->tgrd", p_c.astype(vc.dtype), vc)
    imp = jnp.einsum("tgn,ns->tgs", jnp.sum(p_c, axis=2), sel_map)
    blk = jnp.arange(n_sel)[None, :]
    cur = (q_pos // SEL_BLOCK)[:, None]
    forced = (blk == 0) | (blk == cur) | (blk == cur - 1)
    valid = blk * SEL_BLOCK <= q_pos[:, None]
    bonus = jnp.where(forced, FORCE_BONUS, 0.0)
    imp = jnp.where(valid[:, None, :], imp + bonus[:, None, :], NEG)
    _, idx = lax.top_k(imp, min(N_SEL, n_sel))
    k_eff = idx.shape[-1]
    g_idx = jnp.arange(NSA_KV_HEADS)[None, :, None]

    def gather_blocks(a):
        a = a.reshape(n_sel, SEL_BLOCK, NSA_KV_HEADS, HEAD_DIM).transpose(2, 0, 1, 3)
        return a[g_idx, idx].reshape(tq, NSA_KV_HEADS, k_eff * SEL_BLOCK, HEAD_DIM)

    k_s = gather_blocks(ks)
    v_s = gather_blocks(vs)
    pos_s = (idx[..., None] * SEL_BLOCK + jnp.arange(SEL_BLOCK)).reshape(tq, NSA_KV_HEADS, k_eff * SEL_BLOCK)
    m_s = (pos_s <= q_pos[:, None, None])[:, :, None, :]
    p_s = masked_softmax(jnp.einsum("tgrd,tgld->tgrl", q, k_s) * scale, m_s)
    o_s = jnp.einsum("tgrl,tgld->tgrd", p_s.astype(v_s.dtype), v_s)
    dist = q_pos[:, None] - w_pos[None, :]
    m_w = ((dist >= 0) & (dist <= WINDOW) & (w_pos[None, :] >= 0))[:, None, None, :]
    p_w = masked_softmax(jnp.einsum("tgrd,lgd->tgrl", q, kw) * scale, m_w)
    o_w = jnp.einsum("tgrl,lgd->tgrd", p_w.astype(vw.dtype), vw)
    return gates[..., 0:1] * o_c + gates[..., 1:2] * o_s + gates[..., 2:3] * o_w


def nsa_prompt(z, q_g, k_g, pe, w1, b1, w2):
    n, s, _ = z.shape
    q, cmp_kv, sel_kv, win_kv, gates = nsa_split(z, q_g, k_g)
    kc = rms_norm(compress(cmp_kv[:, :, 0], pe[0], w1[0], b1[0], w2[0]), k_g[0])
    vc = compress(cmp_kv[:, :, 1], pe[1], w1[1], b1[1], w2[1])
    out = nsa_prompt_attend(q, kc, vc, sel_kv, win_kv, gates)
    pages = lambda a: a.reshape(n, s // PAGE_SIZE, PAGE_SIZE, 2, NSA_KV_HEADS, HEAD_DIM)
    wb = min(WINDOW, s)
    return out, pages(cmp_kv), pages(sel_kv), win_kv[:, s - wb:]


_NSA_SEL_TILE = 512


def _softmax_rows(s, mask):
    sm = jnp.where(mask, s, NEG)
    e = jnp.exp(sm - jnp.max(sm, axis=-1, keepdims=True))
    return jnp.where(mask, e / jnp.sum(e, axis=-1, keepdims=True), 0.0)


def _nsa_prompt_kernel(q_ref, gate_ref, kct_ref, vc_ref, smap_ref, kst_ref, vs_ref, kwt_ref, vw_ref, o_ref,
                       m_scr, l_scr, acc_scr):
    qb = pl.program_id(1)
    tq = q_ref.shape[3]
    n_cmp = kct_ref.shape[3]
    n_sel = smap_ref.shape[1]
    tk = kst_ref.shape[4]
    n_top = min(N_SEL, n_sel)
    scale = HEAD_DIM ** -0.5
    s0 = qb * tq
    pos = s0 + lax.broadcasted_iota(jnp.int32, (tq, 1), 0)
    pos3 = jnp.concatenate([pos] * NSA_GROUP, axis=0)
    gates = gate_ref[0]
    cend = lax.broadcasted_iota(jnp.int32, (1, n_cmp), 1) * CMP_STRIDE + (CMP_LEN - 1)
    blk = lax.broadcasted_iota(jnp.int32, (1, n_sel), 1)
    cur = pos // SEL_BLOCK
    forced = (blk == 0) | (blk == cur) | (blk == cur - 1)
    valid = blk * SEL_BLOCK <= pos
    n_tiles = (s0 + tq + tk - 1) // tk
    n_wt = WINDOW // tq + 1
    wt0 = jnp.maximum(qb - WINDOW // tq, 0)
    kposw = wt0 * tq + lax.broadcasted_iota(jnp.int32, (1, n_wt * tq), 1)
    dist = pos3 - kposw
    mw = (dist >= 0) & (dist <= WINDOW)

    for g in range(NSA_KV_HEADS):
        cols = slice(g * HEAD_DIM, (g + 1) * HEAD_DIM)
        qg = q_ref[0, g].reshape(NSA_GROUP * tq, HEAD_DIM)
        qg_bf = qg.astype(_BF16)
        sc = jnp.dot(qg, kct_ref[0, g], precision=_HI, preferred_element_type=_F32) * scale
        p = _softmax_rows(sc, cend <= pos3)
        oc = jnp.dot(p.astype(_BF16), vc_ref[0, g], preferred_element_type=_F32)
        psum = p[0:tq]
        for r in range(1, NSA_GROUP):
            psum = psum + p[r * tq:(r + 1) * tq]
        imp = jnp.dot(psum, smap_ref[...], precision=_HI, preferred_element_type=_F32)
        imp = jnp.where(valid, imp + jnp.where(forced, FORCE_BONUS, 0.0), NEG)
        _, rank_t = _top_rows(imp.T, n_top)
        sel = (rank_t < float(n_top)).astype(_F32).T.astype(_BF16)
        m_scr[...] = jnp.full(m_scr.shape, NEG, _F32)
        l_scr[...] = jnp.zeros(l_scr.shape, _F32)
        acc_scr[...] = jnp.zeros(acc_scr.shape, _F32)

        def body(kt, carry):
            s = jnp.dot(qg_bf, kst_ref[0, g, kt], preferred_element_type=_F32) * scale
            brow = lax.broadcasted_iota(jnp.int32, (n_sel, tk), 0)
            bkey = kt * (tk // SEL_BLOCK) + lax.broadcasted_iota(jnp.int32, (n_sel, tk), 1) // SEL_BLOCK
            selx = jnp.dot(sel, (brow == bkey).astype(_BF16), preferred_element_type=_F32)
            kpos = kt * tk + lax.broadcasted_iota(jnp.int32, (1, tk), 1)
            mk = jnp.where((selx > 0.5) & (kpos <= pos), 1.0, 0.0)
            mk3 = jnp.concatenate([mk] * NSA_GROUP, axis=0) > 0.5
            sm = jnp.where(mk3, s, NEG)
            m_old = m_scr[...]
            m_new = jnp.maximum(m_old, jnp.max(sm, axis=-1, keepdims=True))
            alpha = jnp.exp(m_old - m_new)
            pe = jnp.where(mk3, jnp.exp(sm - m_new), 0.0)
            l_scr[...] = alpha * l_scr[...] + jnp.sum(pe, axis=-1, keepdims=True)
            acc_scr[...] = alpha * acc_scr[...] + jnp.dot(pe.astype(_BF16), vs_ref[0, kt],
                                                          preferred_element_type=_F32)
            m_scr[...] = m_new
            return carry

        lax.fori_loop(0, n_tiles, body, 0)
        osel = acc_scr[:, cols] / l_scr[...]
        kwt = jnp.concatenate([kwt_ref[0, g, wt0 + i] for i in range(n_wt)], axis=1)
        vw = jnp.concatenate([vw_ref[0, wt0 + i] for i in range(n_wt)], axis=0)
        sw = jnp.dot(qg_bf, kwt, preferred_element_type=_F32) * scale
        pw = _softmax_rows(sw, mw)
        ow = jnp.dot(pw.astype(_BF16), vw, preferred_element_type=_F32)[:, cols]
        for r in range(NSA_GROUP):
            c = (g * NSA_GROUP + r) * 3
            rows = slice(r * tq, (r + 1) * tq)
            o_ref[0, g, r] = (gates[:, c:c + 1] * oc[rows] + gates[:, c + 1:c + 2] * osel[rows]
                              + gates[:, c + 2:c + 3] * ow[rows])


def nsa_prompt_attend(q, kc, vc, sel_kv, win_kv, gates):
    n, s = q.shape[:2]
    G, R, HD = NSA_KV_HEADS, NSA_GROUP, HEAD_DIM
    tq, tk = NSA_QBLOCK, _NSA_SEL_TILE
    n_cmp = kc.shape[1]
    ncp = s // CMP_STRIDE
    n_sel = s // SEL_BLOCK
    assert s % tk == 0 and s >= WINDOW + tq
    padc = ((0, 0), (0, ncp - n_cmp), (0, 0), (0, 0))
    qh = q.transpose(0, 2, 3, 1, 4)
    kct = jnp.pad(kc, padc).transpose(0, 2, 3, 1)
    vcb = jnp.pad(vc, padc).transpose(0, 2, 1, 3).astype(_BF16)
    smap = jnp.pad(selection_map(n_cmp, n_sel), ((0, ncp - n_cmp), (0, 0)))
    kst = sel_kv[:, :, 0].astype(_BF16).reshape(n, s // tk, tk, G, HD).transpose(0, 3, 1, 4, 2)
    vs = sel_kv[:, :, 1].astype(_BF16).reshape(n, s // tk, tk, G * HD)
    kwt = win_kv[:, :, 0].astype(_BF16).reshape(n, s // tq, tq, G, HD).transpose(0, 3, 1, 4, 2)
    vw = win_kv[:, :, 1].astype(_BF16).reshape(n, s // tq, tq, G * HD)
    once = pl.Buffered(1)
    out = pl.pallas_call(
        _nsa_prompt_kernel,
        grid=(n, s // tq),
        in_specs=[
            pl.BlockSpec((1, G, R, tq, HD), lambda b, i: (b, 0, 0, i, 0)),
            pl.BlockSpec((1, tq, G * R * 3), lambda b, i: (b, i, 0)),
            pl.BlockSpec((1, G, HD, ncp), lambda b, i: (b, 0, 0, 0)),
            pl.BlockSpec((1, G, ncp, HD), lambda b, i: (b, 0, 0, 0)),
            pl.BlockSpec((ncp, n_sel), lambda b, i: (0, 0)),
            pl.BlockSpec((1, G, s // tk, HD, tk), lambda b, i: (b, 0, 0, 0, 0), pipeline_mode=once),
            pl.BlockSpec((1, s // tk, tk, G * HD), lambda b, i: (b, 0, 0, 0), pipeline_mode=once),
            pl.BlockSpec((1, G, s // tq, HD, tq), lambda b, i: (b, 0, 0, 0, 0), pipeline_mode=once),
            pl.BlockSpec((1, s // tq, tq, G * HD), lambda b, i: (b, 0, 0, 0), pipeline_mode=once),
        ],
        out_specs=pl.BlockSpec((1, G, R, tq, HD), lambda b, i: (b, 0, 0, i, 0)),
        out_shape=jax.ShapeDtypeStruct((n, G, R, s, HD), _F32),
        scratch_shapes=[pltpu.VMEM((R * tq, 1), _F32), pltpu.VMEM((R * tq, 1), _F32),
                        pltpu.VMEM((R * tq, G * HD), _F32)],
        compiler_params=pltpu.CompilerParams(dimension_semantics=("arbitrary", "arbitrary"),
                                             vmem_limit_bytes=VMEM_BYTES_V7X * 3 // 4),
        name="nsa_prompt",
    )(qh, gates.reshape(n, s, G * R * 3), kct, vcb, smap, kst, vs, kwt, vw)
    return out.transpose(0, 3, 1, 2, 4).reshape(n, s, MIX_W)


_BF16_ROWS = 16


def _softmax_cols_aug(sT, mask, v_aug):
    sm = jnp.where(mask, sT, NEG)
    e = jnp.where(mask, jnp.exp(sm - jnp.max(sm, axis=0, keepdims=True)), 0.0)
    acc = lax.dot_general(e.astype(_BF16), v_aug, (((0,), (0,)), ((), ())), preferred_element_type=_F32)
    c = v_aug.shape[1] - LANES
    return acc[:, :c] / acc[:, c:c + 1]


def _nsa_sample_kernel(pt_ref, qbd_ref, kc_ref, vc_ref, smapt_ref, rsum_ref, rexp_ref, gate_ref, *rest,
                       n_pages, past, t_new, n_cmp, n_sel, wb):
    page_refs = rest[:n_pages]
    selnew_ref, win_ref, o_ref, s_scr = rest[n_pages:]
    kw = NSA_KV_HEADS * HEAD_DIM
    scale = HEAD_DIM ** -0.5
    n_top = min(N_SEL, n_sel)
    qbd = qbd_ref[0]
    qbd_bf = qbd.astype(_BF16)
    nl = qbd.shape[1]
    pos = past + lax.broadcasted_iota(jnp.int32, (1, nl), 1) % t_new
    ones_cols = lambda k: jnp.ones((k, LANES), _BF16)

    ncp = kc_ref.shape[1]
    ci = lax.broadcasted_iota(jnp.int32, (ncp, 1), 0)
    mc = (ci * CMP_STRIDE + (CMP_LEN - 1) <= pos) & (ci < n_cmp)
    sc = jnp.dot(kc_ref[0], qbd, precision=_HI, preferred_element_type=_F32) * scale
    scm = jnp.where(mc, sc, NEG)
    e = jnp.exp(scm - jnp.max(scm, axis=0, keepdims=True))
    pc = jnp.where(mc, e / jnp.sum(e, axis=0, keepdims=True), 0.0)
    oc = lax.dot_general(pc.astype(_BF16), vc_ref[0], (((0,), (0,)), ((), ())), preferred_element_type=_F32)
    psum = jnp.dot(pc, rsum_ref[...], precision=_HI, preferred_element_type=_F32)
    imp = jnp.dot(smapt_ref[...], psum, precision=_HI, preferred_element_type=_F32)
    nb, nq = imp.shape
    pos_q = past + lax.broadcasted_iota(jnp.int32, (1, nq), 1) % t_new
    blk = lax.broadcasted_iota(jnp.int32, (nb, 1), 0)
    cur = pos_q // SEL_BLOCK
    forced = (blk == 0) | (blk == cur) | (blk == cur - 1)
    imp = jnp.where(blk * SEL_BLOCK <= pos_q, imp + jnp.where(forced, FORCE_BONUS, 0.0), NEG)
    imp = jnp.where(blk < n_sel, imp, -jnp.inf)
    _, rank = _top_rows(imp, n_top)
    sel = jnp.dot((rank < float(n_top)).astype(_BF16), rexp_ref[...], preferred_element_type=_F32)

    rowi = lax.broadcasted_iota(jnp.int32, (PAGE_SIZE, 1), 0)
    per_page = PAGE_SIZE // SEL_BLOCK
    for p in range(n_pages):
        kp = page_refs[p][0, :, 0:kw].astype(_BF16)
        s = jnp.dot(kp, qbd_bf, preferred_element_type=_F32) * scale
        pick = sel[p * per_page:p * per_page + 1, :]
        for b in range(1, per_page):
            pick = jnp.where(rowi >= b * SEL_BLOCK, sel[p * per_page + b:p * per_page + b + 1, :], pick)
        ok = (pick > 0.5) & (p * PAGE_SIZE + rowi <= pos)
        s_scr[p * PAGE_SIZE:(p + 1) * PAGE_SIZE, :] = jnp.where(ok, s, NEG)
    n_new = selnew_ref.shape[1]
    rown = lax.broadcasted_iota(jnp.int32, (n_new, 1), 0)
    kn = selnew_ref[0, :, 0:kw].astype(_BF16).astype(_F32)
    s = jnp.dot(kn, qbd_bf.astype(_F32), precision=_HI, preferred_element_type=_F32) * scale
    ok = (sel[past // SEL_BLOCK:past // SEL_BLOCK + 1, :] > 0.5) & (past + rown <= pos)
    s_scr[past:past + n_new, :] = jnp.where(ok, s, NEG)
    mx = jnp.max(s_scr[...], axis=0, keepdims=True)
    acc = jnp.zeros((nl, kw + LANES), _F32)
    for p in range(n_pages):
        sp = s_scr[p * PAGE_SIZE:(p + 1) * PAGE_SIZE, :]
        ep = jnp.where(sp > 0.5 * NEG, jnp.exp(sp - mx), 0.0).astype(_BF16)
        v_aug = jnp.concatenate([page_refs[p][0, :, kw:2 * kw].astype(_BF16), ones_cols(PAGE_SIZE)], axis=1)
        acc = acc + lax.dot_general(ep, v_aug, (((0,), (0,)), ((), ())), preferred_element_type=_F32)
    sp = s_scr[past:past + n_new, :]
    ep = jnp.where(sp > 0.5 * NEG, jnp.exp(sp - mx), 0.0).astype(_BF16)
    v_aug = jnp.concatenate([selnew_ref[0, :, kw:2 * kw].astype(_BF16), ones_cols(n_new)], axis=1)
    acc = acc + lax.dot_general(ep, v_aug, (((0,), (0,)), ((), ())), preferred_element_type=_F32)
    osel = acc[:, :kw] / acc[:, kw:kw + 1]

    n_w = win_ref.shape[1]
    wpos = past - wb + lax.broadcasted_iota(jnp.int32, (n_w, 1), 0)
    dist = pos - wpos
    mw = (dist >= 0) & (dist <= WINDOW) & (wpos >= 0)
    sw = jnp.dot(win_ref[0, :, 0:kw].astype(_BF16), qbd_bf, preferred_element_type=_F32) * scale
    v_aug = jnp.concatenate([win_ref[0, :, kw:2 * kw].astype(_BF16), ones_cols(n_w)], axis=1)
    ow = _softmax_cols_aug(sw, mw, v_aug)

    g = gate_ref[0]
    o_ref[0] = g[:, 0:1] * oc + g[:, 1:2] * osel + g[:, 2:3] * ow


def nsa_sample_attend(q, kc, vc, pool_sel, page_table, sel_new, win_all, gates, wb):
    n, t_new = q.shape[:2]
    G, R, HD = NSA_KV_HEADS, NSA_GROUP, HEAD_DIM
    n_pages = page_table.shape[1]
    past = n_pages * PAGE_SIZE
    n_cmp = kc.shape[1]
    n_sel = -(-(past + t_new) // SEL_BLOCK)
    assert past % SEL_BLOCK == 0 and t_new <= SEL_BLOCK and PAGE_SIZE % SEL_BLOCK == 0
    kw = G * HD
    nl = G * R * t_new
    up = lambda v, m: -(-v // m) * m
    ncp, nbp = up(n_cmp, LANES), up(n_sel, 8)
    n_new, n_w = up(t_new, _BF16_ROWS), up(wb + t_new, _BF16_ROWS)
    qbd = jnp.einsum("ntgrd,gh->ngdhrt", q, jnp.eye(G, dtype=q.dtype)).reshape(n, kw, nl)
    kcp = jnp.pad(kc.reshape(n, n_cmp, kw), ((0, 0), (0, ncp - n_cmp), (0, 0)))
    vcp = jnp.pad(vc.reshape(n, n_cmp, kw), ((0, 0), (0, ncp - n_cmp), (0, 0))).astype(_BF16)
    smapt = jnp.pad(selection_map(n_cmp, n_sel).T, ((0, nbp - n_sel), (0, ncp - n_cmp)))
    rsum = np.zeros((G, R, t_new, G, t_new), np.float32)
    for g in range(G):
        for t in range(t_new):
            rsum[g, :, t, g, t] = 1.0
    rsum = rsum.reshape(nl, G * t_new)
    gates3 = gates.transpose(0, 2, 3, 1, 4).reshape(n, nl, 3)
    pages = pool_sel.reshape(pool_sel.shape[0], PAGE_SIZE, 2 * kw)
    seln = jnp.pad(sel_new.reshape(n, t_new, 2 * kw), ((0, 0), (0, n_new - t_new), (0, 0)))
    win = jnp.pad(win_all.reshape(n, wb + t_new, 2 * kw), ((0, 0), (0, n_w - wb - t_new), (0, 0)))
    whole = lambda *shape: pl.BlockSpec(shape, lambda b, pt: (0,) * len(shape))
    per_seq = lambda *shape: pl.BlockSpec((1,) + shape, lambda b, pt: (b,) + (0,) * len(shape))
    page_spec = lambda p: pl.BlockSpec((1, PAGE_SIZE, 2 * kw), lambda b, pt: (pt[b, p], 0, 0))
    out = pl.pallas_call(
        functools.partial(_nsa_sample_kernel, n_pages=n_pages, past=past, t_new=t_new, n_cmp=n_cmp, n_sel=n_sel,
                          wb=wb),
        grid_spec=pltpu.PrefetchScalarGridSpec(
            num_scalar_prefetch=1,
            grid=(n,),
            in_specs=[per_seq(kw, nl), per_seq(ncp, kw), per_seq(ncp, kw), whole(nbp, ncp),
                      whole(nl, G * t_new), whole(G * t_new, nl), per_seq(nl, 3)]
                     + [page_spec(p) for p in range(n_pages)]
                     + [per_seq(n_new, 2 * kw), per_seq(n_w, 2 * kw)],
            out_specs=per_seq(nl, kw),
            scratch_shapes=[pltpu.VMEM((past + n_new, nl), _F32)],
        ),
        out_shape=jax.ShapeDtypeStruct((n, nl, kw), _F32),
        compiler_params=pltpu.CompilerParams(dimension_semantics=("arbitrary",),
                                             vmem_limit_bytes=VMEM_BYTES_V7X // 2),
        name="nsa_sample",
    )(page_table, qbd, kcp, vcp, smapt, jnp.asarray(rsum), jnp.asarray(rsum.T).astype(_BF16), gates3,
      *([pages] * n_pages), seln, win)
    o = out.reshape(n, G, R, t_new, G, HD)
    o = jnp.stack([o[:, g, :, :, g, :] for g in range(G)], axis=1)
    return o.transpose(0, 3, 1, 2, 4).reshape(n, t_new, MIX_W)


def nsa_sample(z, pool_cmp, pool_sel, win_buf, page_table, q_g, k_g, pe, w1, b1, w2):
    n, t_new, _ = z.shape
    past = page_table.shape[1] * PAGE_SIZE
    G, HD = NSA_KV_HEADS, HEAD_DIM
    q, cmp_new, sel_new, win_new, gates = nsa_split(z, q_g, k_g)
    win_all = jnp.concatenate([win_buf, win_new], axis=1)
    wb = win_buf.shape[1]
    n_ch = (past + t_new) // CMP_STRIDE
    assert PAGE_SIZE % CMP_STRIDE == 0 and n_ch * CMP_STRIDE <= past
    per_page = PAGE_SIZE // CMP_STRIDE
    pool6 = pool_cmp.reshape(pool_cmp.shape[0], per_page, CMP_STRIDE, 2, G, HD)
    w1r = w1.reshape(2, 2, CMP_STRIDE, HD, CMP_HIDDEN)
    a_pool = jnp.einsum("pcskgd,kjsdh->pckjgh", pool6, w1r)
    a_seq = a_pool[page_table].reshape(n, past // CMP_STRIDE, 2, 2, G, CMP_HIDDEN)[:, :n_ch]
    bias = jnp.einsum("kx,kxh->kh", pe.reshape(2, -1), w1) + b1
    pre = a_seq[:, :-1, :, 0] + a_seq[:, 1:, :, 1] + bias[None, None, :, None, :]
    comp = jnp.einsum("nckgh,khd->nckgd", jax.nn.gelu(pre, approximate=False), w2)
    kc = rms_norm(comp[:, :, 0], k_g[0])
    vc = comp[:, :, 1]
    out = nsa_sample_attend(q, kc, vc, pool_sel, page_table, sel_new, win_all, gates, wb)
    return out, cmp_new, sel_new, win_all[:, -wb:]


def shortconv_mix(z, ctx, w, b):
    bg = z[..., :MIX_W]
    cg = z[..., MIX_W:2 * MIX_W]
    v = z[..., 2 * MIX_W:3 * MIX_W]
    u_ext = jnp.concatenate([ctx, cg * v], axis=1)
    y = bg * causal_dwconv(u_ext, w, b)
    return y, u_ext[:, -(CONV_B_WIDTH - 1):]


def conformer_mix(z, ctx, w, b, ln_g, ln_b):
    u = z[..., :MIX_W] * jax.nn.sigmoid(z[..., MIX_W:2 * MIX_W])
    u_ext = jnp.concatenate([ctx, u], axis=1)
    y = jax.nn.silu(layer_norm(causal_dwconv(u_ext, w, b), ln_g, ln_b))
    return y, u_ext[:, -(CONV_C_WIDTH - 1):]


_CAND = [(j1, j2) for j1 in range(PEER_TOPK) for j2 in range(PEER_TOPK) if (j1 + 1) * (j2 + 1) <= PEER_TOPK]
_N_CAND = len(_CAND)
_CAND_ROWS = -(-_N_CAND // 8) * 8
_PEER_SEL_TILE = 256
_PEER_TOK_TILE = 512
_PEER_EXP_BLOCK = 1024
_PEER_SUB_BLOCK = 256
_PEER_ROW_GROUP = 32


def _cand_tables():
    j1 = np.full((_CAND_ROWS, LANES), -1.0, np.float32)
    j2 = np.full((_CAND_ROWS, LANES), -1.0, np.float32)
    flat = np.full((_CAND_ROWS, LANES), 1.0e6, np.float32)
    for r, (a, b) in enumerate(_CAND):
        j1[r], j2[r], flat[r] = a, b, a * PEER_TOPK + b
    return np.stack([j1, j2, flat])


def _gelu_erf(x):
    return 0.5 * x * (1.0 + lax.erf(x * (2.0 ** -0.5)))


def _top_rows(s, n_top):
    (vals, rank), = _top_rows_many([s], n_top)
    return vals, rank


def _top_rows_many(arrays, n_top):
    k, l = arrays[0].shape
    kidx = lax.broadcasted_iota(jnp.int32, (k, l), 0).astype(_F32)
    jrow = lax.broadcasted_iota(jnp.int32, (n_top, l), 0)
    work = list(arrays)
    rank = [jnp.full((k, l), float(n_top), _F32) for _ in arrays]
    vals = [jnp.zeros((n_top, l), _F32) for _ in arrays]
    for j in range(n_top):
        for i in range(len(arrays)):
            m = jnp.max(work[i], axis=0, keepdims=True)
            first = jnp.min(jnp.where(work[i] == m, kidx, float(k)), axis=0, keepdims=True)
            sel = kidx == first
            rank[i] = jnp.where(sel, float(j), rank[i])
            work[i] = jnp.where(sel, -jnp.inf, work[i])
            vals[i] = jnp.where(jrow == j, m, vals[i])
    return list(zip(vals, rank))


def _peer_select_kernel(x_ref, g_ref, wq_ref, sk_ref, ct_ref, ht_ref, p1_ref, c1_ref, p2_ref, r2_ref, h_scr):
    hd = pl.program_id(1)

    @pl.when(hd == 0)
    def _():
        x = x_ref[...]
        h = x * lax.rsqrt(jnp.mean(x * x, axis=-1, keepdims=True) + EPS) * g_ref[...]
        h_scr[...] = h
        ht_ref[...] = h.T.astype(_BF16)

    q = jnp.dot(h_scr[...], wq_ref[...], precision=_HI, preferred_element_type=_F32)
    nt = (((1,), (1,)), ((), ()))
    half = PEER_DK // 2
    s1_all = lax.dot_general(sk_ref[0, 0], q[:, :half], nt, precision=_HI, preferred_element_type=_F32)
    s2_all = lax.dot_general(sk_ref[0, 1], q[:, half:], nt, precision=_HI, preferred_element_type=_F32)
    cj1, cj2, cflat = ct_ref[0], ct_ref[1], ct_ref[2]
    for c in range(x_ref.shape[0] // LANES):
        sl = slice(c * LANES, (c + 1) * LANES)
        s1, s2 = s1_all[:, sl], s2_all[:, sl]
        (v1, rank1), (v2, rank2) = _top_rows_many([s1, s2], PEER_TOPK)
        a1 = jnp.zeros((_CAND_ROWS, LANES), _F32)
        a2 = jnp.zeros((_CAND_ROWS, LANES), _F32)
        for j in range(PEER_TOPK):
            a1 = jnp.where(cj1 == float(j), v1[j:j + 1, :], a1)
            a2 = jnp.where(cj2 == float(j), v2[j:j + 1, :], a2)
        cand = jnp.where(cj1 >= 0.0, a1 + a2, -jnp.inf)
        work = cand
        picked = jnp.zeros((_CAND_ROWS, LANES), _F32)
        for _ in range(PEER_TOPK):
            m = jnp.max(work, axis=0, keepdims=True)
            first = jnp.min(jnp.where(work == m, cflat, 2.0e6), axis=0, keepdims=True)
            sel = cflat == first
            picked = jnp.where(sel, 1.0, picked)
            work = jnp.where(sel, -jnp.inf, work)
        top = v1[0:1, :] + v2[0:1, :]
        z = jnp.sum(jnp.where(picked > 0.0, jnp.exp(cand - top), 0.0), axis=0, keepdims=True)
        c1 = jnp.zeros((PEER_KEYS, LANES), _F32)
        for j in range(PEER_TOPK):
            cnt_j = jnp.sum(jnp.where(cj1 == float(j), picked, 0.0), axis=0, keepdims=True)
            c1 = jnp.where(rank1 == float(j), cnt_j, c1)
        p1_ref[0, :, sl] = jnp.exp(s1 - v1[0:1, :]) / z
        c1_ref[0, :, sl] = c1
        p2_ref[0, :, sl] = jnp.exp(s2 - v2[0:1, :])
        r2_ref[0, :, sl] = rank2


def _peer_select(x, g, w_q, sub_keys):
    n, d = x.shape
    ts = _PEER_SEL_TILE
    assert n % ts == 0
    fac = jax.ShapeDtypeStruct((PEER_HEADS, PEER_KEYS, n), _F32)
    fac_spec = pl.BlockSpec((1, PEER_KEYS, ts), lambda i, h: (h, 0, i))
    return pl.pallas_call(
        _peer_select_kernel,
        grid=(n // ts, PEER_HEADS),
        in_specs=[
            pl.BlockSpec((ts, d), lambda i, h: (i, 0)),
            pl.BlockSpec((1, d), lambda i, h: (0, 0)),
            pl.BlockSpec((d, PEER_DK), lambda i, h: (0, h)),
            pl.BlockSpec((1, 2, PEER_KEYS, PEER_DK // 2), lambda i, h: (h, 0, 0, 0)),
            pl.BlockSpec((3, _CAND_ROWS, LANES), lambda i, h: (0, 0, 0)),
        ],
        out_specs=[pl.BlockSpec((d, ts), lambda i, h: (0, i)), fac_spec, fac_spec, fac_spec, fac_spec],
        out_shape=[jax.ShapeDtypeStruct((d, n), _BF16), fac, fac, fac, fac],
        scratch_shapes=[pltpu.VMEM((ts, d), _F32)],
        compiler_params=pltpu.CompilerParams(dimension_semantics=("arbitrary", "arbitrary"),
                                             vmem_limit_bytes=VMEM_BYTES_V7X * 3 // 4),
        name="peer_select",
    )(x, g.reshape(1, d), w_q, sub_keys, jnp.asarray(_cand_tables()))


def _peer_dense_kernel(x_ref, ht_ref, u_ref, vt_ref, p1_ref, c1_ref, p2_ref, r2_ref, o_ref,
                       acc_ref, *scratch):
    j = pl.program_id(1)

    @pl.when(j == 0)
    def _():
        acc_ref[...] = jnp.zeros_like(acc_ref)

    eb, tt = u_ref.shape[0], ht_ref.shape[1]
    sub = _PEER_SUB_BLOCK
    n_sub = eb // sub
    st_refs, wact_refs = scratch[:2], scratch[2:]
    half = tt // 2
    d = vt_ref.shape[0]
    n_out = d // sub

    def scores(k, nh):
        lanes = slice(nh * half, (nh + 1) * half)
        st_refs[k % 2][:, lanes] = jnp.dot(u_ref[k * sub:(k + 1) * sub, :], ht_ref[:, lanes],
                                           preferred_element_type=_F32)

    def output_piece(k, q):
        r = slice(q * sub, (q + 1) * sub)
        acc_ref[r, :] += jnp.dot(vt_ref[r, k * sub:(k + 1) * sub], wact_refs[k % 2][...],
                                 preferred_element_type=_F32)

    def build_chunk(k, al, c, a):
        lanes = slice(c * LANES, (c + 1) * LANES)
        c1 = [c1_ref[h, pl.ds(a, 1), :][:, lanes] for h in range(PEER_HEADS)]
        p1 = [p1_ref[h, pl.ds(a, 1), :][:, lanes] for h in range(PEER_HEADS)]
        for rg in range(PEER_KEYS // _PEER_ROW_GROUP):
            b = slice(rg * _PEER_ROW_GROUP, (rg + 1) * _PEER_ROW_GROUP)
            brows = slice(al * PEER_KEYS + b.start, al * PEER_KEYS + b.stop)
            w = jnp.zeros((_PEER_ROW_GROUP, LANES), _F32)
            for h in range(PEER_HEADS):
                w = w + jnp.where(r2_ref[h, b, lanes] < c1[h], p2_ref[h, b, lanes], 0.0) * p1[h]
            wact_refs[k % 2][brows, lanes] = (w * _gelu_erf(st_refs[k % 2][brows, lanes])).astype(_BF16)

    n_al, n_c = sub // PEER_KEYS, tt // LANES
    n_chunks = n_al * n_c
    scores(0, 0)
    scores(0, 1)
    for k in range(n_sub):
        i = 0
        for al in range(n_al):
            a = j * (eb // PEER_KEYS) + k * n_al + al
            for c in range(n_c):
                build_chunk(k, al, c, a)
                if k + 1 < n_sub and i % (n_chunks // 2) == 0:
                    scores(k + 1, i // (n_chunks // 2))
                if k >= 1 and i % (n_chunks // n_out) == n_chunks // n_out - 1:
                    output_piece(k - 1, i // (n_chunks // n_out))
                i += 1
    for q in range(n_out):
        output_piece(n_sub - 1, q)

    @pl.when(j == pl.num_programs(1) - 1)
    def _():
        o_ref[...] = x_ref[...] + acc_ref[...].T


def _peer_dense(x, ht, u_bf, vt_bf, p1, c1, p2, r2):
    n, d = x.shape
    tt, eb = _PEER_TOK_TILE, _PEER_EXP_BLOCK
    assert n % tt == 0 and N_EXPERTS % eb == 0
    fac_spec = pl.BlockSpec((PEER_HEADS, PEER_KEYS, tt), lambda i, j: (0, 0, i))
    return pl.pallas_call(
        _peer_dense_kernel,
        grid=(n // tt, N_EXPERTS // eb),
        in_specs=[
            pl.BlockSpec((tt, d), lambda i, j: (i, 0)),
            pl.BlockSpec((d, tt), lambda i, j: (0, i)),
            pl.BlockSpec((eb, d), lambda i, j: (j, 0)),
            pl.BlockSpec((d, eb), lambda i, j: (0, j)),
            fac_spec, fac_spec, fac_spec, fac_spec,
        ],
        out_specs=pl.BlockSpec((tt, d), lambda i, j: (i, 0)),
        out_shape=jax.ShapeDtypeStruct((n, d), _F32),
        scratch_shapes=([pltpu.VMEM((d, tt), _F32)]
                        + [pltpu.VMEM((_PEER_SUB_BLOCK, tt), _F32)] * 2
                        + [pltpu.VMEM((_PEER_SUB_BLOCK, tt), _BF16)] * 2),
        compiler_params=pltpu.CompilerParams(dimension_semantics=("arbitrary", "arbitrary"),
                                             vmem_limit_bytes=VMEM_BYTES_V7X * 3 // 4),
        name="peer_dense",
    )(x, ht, u_bf, vt_bf, p1, c1, p2, r2)


def peer_residual(x, g, w_q, sub_keys, u_tab, v_tab):
    ht, p1, c1, p2, r2 = _peer_select(x, g, w_q, sub_keys)
    return _peer_dense(x, ht, u_tab.astype(_BF16), v_tab.T.astype(_BF16), p1, c1, p2, r2)


def kernel(x_prompt, x_sample, cache_a_cmp_kv, cache_a_sel_kv, cache_a_win_kv, state_b_conv, state_c_conv,
           cache_mem_kv, page_table, mem_prompt, norm_mix_g, norm_mem_g, w_mem_kv, mem_q_norm_g, mem_k_norm_g,
           w_out, norm_ffn_g, peer_w_q, peer_sub_keys, peer_u, peer_v, a_w_in, a_q_norm_g, a_k_norm_g,
           a_cmp_pe, a_cmp_w1, a_cmp_b1, a_cmp_w2, b_w_in, b_conv_w, b_conv_b, c_w_in, c_conv_w, c_conv_b,
           c_ln_g, c_ln_b):
    (n_p, s_len), (n_s, t_new) = x_prompt.shape[:2], x_sample.shape[:2]
    n_tok_p = n_p * s_len
    x_all = jnp.concatenate([x_prompt.reshape(-1, D_MODEL), x_sample.reshape(-1, D_MODEL)], axis=0)
    p_cmp, p_sel, p_win, p_cb, p_cc, p_mem = [], [], [], [], [], []
    s_cmp, s_sel, s_win, s_cb, s_cc = [], [], [], [], []
    for i in range(DEPTH):
        kind, li = i % N_MIXERS, i // N_MIXERS
        w_in = (a_w_in, b_w_in, c_w_in)[kind][li]
        z_all = norm_proj(x_all, norm_mix_g[i], w_in)
        zp = z_all[:n_tok_p].reshape(n_p, s_len, -1)
        zs = z_all[n_tok_p:].reshape(n_s, t_new, -1)
        mem_cols = slice(w_in.shape[1] - MEM_W, w_in.shape[1])
        if kind == 0:
            nsa_w = (a_q_norm_g[li], a_k_norm_g[li], a_cmp_pe[li], a_cmp_w1[li], a_cmp_b1[li], a_cmp_w2[li])
            mp, c_p, sl_p, w_p = nsa_prompt(zp, *nsa_w)
            ms, c_s, sl_s, w_s = nsa_sample(zs, cache_a_cmp_kv[li], cache_a_sel_kv[li], cache_a_win_kv[li],
                                            page_table, *nsa_w)
            p_cmp.append(c_p)
            p_sel.append(sl_p)
            p_win.append(w_p)
            s_cmp.append(c_s)
            s_sel.append(sl_s)
            s_win.append(w_s)
        elif kind == 1:
            mp, st_p = shortconv_mix(zp, jnp.zeros((n_p, CONV_B_WIDTH - 1, MIX_W), zp.dtype), b_conv_w[li], b_conv_b[li])
            ms, st_s = shortconv_mix(zs, state_b_conv[li], b_conv_w[li], b_conv_b[li])
            p_cb.append(st_p)
            s_cb.append(st_s)
        else:
            mp, st_p = conformer_mix(zp, jnp.zeros((n_p, CONV_C_WIDTH - 1, MIX_W), zp.dtype),
                                     c_conv_w[li], c_conv_b[li], c_ln_g[li], c_ln_b[li])
            ms, st_s = conformer_mix(zs, state_c_conv[li], c_conv_w[li], c_conv_b[li], c_ln_g[li], c_ln_b[li])
            p_cc.append(st_p)
            s_cc.append(st_s)
        mkv_p = memory_kv(mem_prompt, norm_mem_g[i], w_mem_kv[i], mem_k_norm_g[i])
        p_mem.append(mkv_p)
        op = jnp.concatenate([mp, memory_attend(zp[..., mem_cols], mkv_p, mem_q_norm_g[i])], axis=-1)
        os_ = jnp.concatenate([ms, memory_attend(zs[..., mem_cols], cache_mem_kv[i], mem_q_norm_g[i])], axis=-1)
        o_all = jnp.concatenate([op.reshape(-1, D_MODEL), os_.reshape(-1, D_MODEL)], axis=0)
        x_all = out_proj_residual(o_all, w_out[i], x_all)
        x_all = peer_residual(x_all, norm_ffn_g[i], peer_w_q[i], peer_sub_keys[i], peer_u[i], peer_v[i])
    xp = x_all[:n_tok_p].reshape(x_prompt.shape)
    xs = x_all[n_tok_p:].reshape(x_sample.shape)
    return (xp, xs, jnp.stack(p_cmp), jnp.stack(p_sel), jnp.stack(p_win), jnp.stack(p_cb), jnp.stack(p_cc),
            jnp.stack(p_mem), jnp.stack(s_cmp), jnp.stack(s_sel), jnp.stack(s_win), jnp.stack(s_cb), jnp.stack(s_cc))
```

```python
import functools

import numpy as np
import jax
import jax.numpy as jnp
from jax import lax
from jax.experimental import pallas as pl
from jax.experimental.pallas import tpu as pltpu

D_MODEL = 1024
BATCH = 2
SEQ = 8192
DEPTH = 4
DEC_BATCH = 128
DEC_SEQ = 8
PAST_LEN = 2048
PAGE_SIZE = 128

N_MIXERS = 3
HEAD_DIM = 64
MIX_W = 3 * D_MODEL // 4
MEM_W = D_MODEL - MIX_W
MEM_HEADS = MEM_W // HEAD_DIM
N_MEM = 256
NSA_HEADS = MIX_W // HEAD_DIM
NSA_KV_HEADS = 4
NSA_GROUP = NSA_HEADS // NSA_KV_HEADS
NSA_KV_W = 2 * NSA_KV_HEADS * HEAD_DIM
CMP_STRIDE = 16
CMP_LEN = 2 * CMP_STRIDE
CMP_HIDDEN = 64
SEL_BLOCK = 64
N_SEL = 16
WINDOW = 512
NSA_QBLOCK = 128
FORCE_BONUS = 1.0e4
A_IN_W = MIX_W + 3 * NSA_KV_W + 3 * NSA_HEADS + MEM_W
CONV_B_WIDTH = 3
B_IN_W = 3 * MIX_W + MEM_W
CONV_C_WIDTH = 31
C_IN_W = 2 * MIX_W + MEM_W
PEER_HEADS = 8
PEER_KEYS = 128
N_EXPERTS = PEER_KEYS * PEER_KEYS
PEER_TOPK = 16
PEER_DK = 256
EPS = 1e-6
NEG = -1e30

LANES = 128
VMEM_BYTES_V7X = 64 * 1024 * 1024

_F32 = jnp.float32
_BF16 = jnp.bfloat16
_HI = lax.Precision.HIGHEST


def rms_norm(x, g):
    xf = x.astype(jnp.float32)
    y = xf * lax.rsqrt(jnp.mean(xf * xf, axis=-1, keepdims=True) + EPS)
    return (y * g.astype(jnp.float32)).astype(x.dtype)


def layer_norm(x, g, b):
    xf = x.astype(jnp.float32)
    mu = jnp.mean(xf, axis=-1, keepdims=True)
    var = jnp.mean(jnp.square(xf - mu), axis=-1, keepdims=True)
    y = (xf - mu) * lax.rsqrt(var + EPS)
    return (y * g.astype(jnp.float32) + b.astype(jnp.float32)).astype(x.dtype)


def masked_softmax(s, mask):
    p = jax.nn.softmax(jnp.where(mask, s.astype(jnp.float32), NEG), axis=-1)
    return jnp.where(mask, p, 0.0)


def causal_dwconv(u_ext, w, b):
    y = lax.conv_general_dilated(u_ext, w[:, None, :].astype(u_ext.dtype), window_strides=(1,),
                                 padding="VALID", dimension_numbers=("NWC", "WIO", "NWC"),
                                 feature_group_count=u_ext.shape[-1])
    return y + b


_PROJ_ROWS = 512
_PROJ_MAX_COLS = 1024


def _norm_proj_kernel(x_ref, g_ref, w_ref, o_ref, h_scr):
    @pl.when(pl.program_id(1) == 0)
    def _():
        x = x_ref[...]
        h = x * lax.rsqrt(jnp.mean(x * x, axis=-1, keepdims=True) + EPS) * g_ref[...]
        h_scr[...] = h.astype(_BF16)

    o_ref[...] = jnp.dot(h_scr[...], w_ref[...], preferred_element_type=_F32)


def norm_proj(x, g, w):
    n, d = x.shape
    m = w.shape[1]
    groups = -(-m // LANES)
    per_tile = max(k for k in range(1, _PROJ_MAX_COLS // LANES + 1) if groups % k == 0)
    tn, mp = per_tile * LANES, groups * LANES
    rows = min(_PROJ_ROWS, n)
    assert n % rows == 0
    wp = jnp.pad(w, ((0, 0), (0, mp - m))).astype(_BF16)
    return pl.pallas_call(
        _norm_proj_kernel,
        grid=(n // rows, mp // tn),
        in_specs=[pl.BlockSpec((rows, d), lambda i, j: (i, 0)),
                  pl.BlockSpec((1, d), lambda i, j: (0, 0)),
                  pl.BlockSpec((d, tn), lambda i, j: (0, j))],
        out_specs=pl.BlockSpec((rows, tn), lambda i, j: (i, j)),
        out_shape=jax.ShapeDtypeStruct((n, mp), _F32),
        scratch_shapes=[pltpu.VMEM((rows, d), _BF16)],
        compiler_params=pltpu.CompilerParams(dimension_semantics=("arbitrary", "arbitrary")),
        name="norm_proj",
    )(x, g.reshape(1, d), wp)


def _out_proj_kernel(o_ref, w_ref, x_ref, y_ref):
    y_ref[...] = x_ref[...] + jnp.dot(o_ref[...].astype(_BF16), w_ref[...], preferred_element_type=_F32)


def out_proj_residual(o, w, x):
    n, d = x.shape
    rows = min(_PROJ_ROWS, n)
    assert n % rows == 0 and w.shape == (d, d)
    row_spec = pl.BlockSpec((rows, d), lambda i: (i, 0))
    return pl.pallas_call(
        _out_proj_kernel,
        grid=(n // rows,),
        in_specs=[row_spec, pl.BlockSpec((d, d), lambda i: (0, 0)), row_spec],
        out_specs=row_spec,
        out_shape=jax.ShapeDtypeStruct((n, d), _F32),
        compiler_params=pltpu.CompilerParams(dimension_semantics=("arbitrary",)),
        name="out_proj",
    )(o, w.astype(_BF16), x)


def memory_kv(mem, g_norm, w_kv, k_g):
    n, m, _ = mem.shape
    kv = norm_proj(mem.reshape(n * m, -1), g_norm, w_kv)[:, :w_kv.shape[1]].reshape(n, m, 2, MEM_HEADS, HEAD_DIM)
    return jnp.stack([rms_norm(kv[:, :, 0], k_g), kv[:, :, 1]], axis=2)


def memory_attend(zq, mkv, q_g):
    n, t, _ = zq.shape
    q = rms_norm(zq.reshape(n, t, MEM_HEADS, HEAD_DIM), q_g)
    s = jnp.einsum("nthd,nmhd->nhtm", q, mkv[:, :, 0]) * (HEAD_DIM ** -0.5)
    p = jax.nn.softmax(s.astype(jnp.float32), axis=-1).astype(zq.dtype)
    return jnp.einsum("nhtm,nmhd->nthd", p, mkv[:, :, 1]).reshape(n, t, MEM_W)


def nsa_split(z, q_g, k_g):
    n, t, _ = z.shape
    G, R, HD = NSA_KV_HEADS, NSA_GROUP, HEAD_DIM
    q = rms_norm(z[..., :MIX_W].reshape(n, t, G, R, HD), q_g)
    o = MIX_W
    cmp_kv = z[..., o:o + NSA_KV_W].reshape(n, t, 2, G, HD)
    sel = z[..., o + NSA_KV_W:o + 2 * NSA_KV_W].reshape(n, t, 2, G, HD)
    win = z[..., o + 2 * NSA_KV_W:o + 3 * NSA_KV_W].reshape(n, t, 2, G, HD)
    sel_kv = jnp.stack([rms_norm(sel[:, :, 0], k_g[1]), sel[:, :, 1]], axis=2)
    win_kv = jnp.stack([rms_norm(win[:, :, 0], k_g[2]), win[:, :, 1]], axis=2)
    o = o + 3 * NSA_KV_W
    gates = jax.nn.sigmoid(z[..., o:o + 3 * NSA_HEADS]).reshape(n, t, G, R, 3)
    return q, cmp_kv, sel_kv, win_kv, gates


def compress(rows, pe, w1, b1, w2):
    n, length, G, HD = rows.shape
    n_ch = length // CMP_STRIDE
    ch = rows[:, :n_ch * CMP_STRIDE].reshape(n, n_ch, CMP_STRIDE, G, HD)
    a = jnp.einsum("ncsgd,ksdh->nckgh", ch, w1.reshape(2, CMP_STRIDE, HD, CMP_HIDDEN))
    bias = pe.reshape(-1) @ w1 + b1
    pre = a[:, :-1, 0] + a[:, 1:, 1] + bias
    return jax.nn.gelu(pre, approximate=False) @ w2


def selection_map(n_cmp, n_sel):
    start = np.arange(n_cmp)[:, None] * CMP_STRIDE
    blk = np.arange(n_sel)[None, :] * SEL_BLOCK
    m = (start <= blk + SEL_BLOCK - 1) & (start + CMP_LEN - 1 >= blk)
    return jnp.asarray(m.astype(np.float32))


def nsa_block(q, q_pos, kc, vc, c_end, ks, vs, kw, vw, w_pos, gates, sel_map):
    scale = HEAD_DIM ** -0.5
    tq = q.shape[0]
    n_sel = sel_map.shape[1]
    m_c = (c_end[None, :] <= q_pos[:, None])[:, None, None, :]
    p_c = masked_softmax(jnp.einsum("tgrd,ngd->tgrn", q, kc) * scale, m_c)
    o_c = jnp.einsum("tgrn,ngd->tgrd", p_c.astype(vc.dtype), vc)
    imp = jnp.einsum("tgn,ns->tgs", jnp.sum(p_c, axis=2), sel_map)
    blk = jnp.arange(n_sel)[None, :]
    cur = (q_pos // SEL_BLOCK)[:, None]
    forced = (blk == 0) | (blk == cur) | (blk == cur - 1)
    valid = blk * SEL_BLOCK <= q_pos[:, None]
    bonus = jnp.where(forced, FORCE_BONUS, 0.0)
    imp = jnp.where(valid[:, None, :], imp + bonus[:, None, :], NEG)
    _, idx = lax.top_k(imp, min(N_SEL, n_sel))
    k_eff = idx.shape[-1]
    g_idx = jnp.arange(NSA_KV_HEADS)[None, :, None]

    def gather_blocks(a):
        a = a.reshape(n_sel, SEL_BLOCK, NSA_KV_HEADS, HEAD_DIM).transpose(2, 0, 1, 3)
        return a[g_idx, idx].reshape(tq, NSA_KV_HEADS, k_eff * SEL_BLOCK, HEAD_DIM)

    k_s = gather_blocks(ks)
    v_s = gather_blocks(vs)
    pos_s = (idx[..., None] * SEL_BLOCK + jnp.arange(SEL_BLOCK)).reshape(tq, NSA_KV_HEADS, k_eff * SEL_BLOCK)
    m_s = (pos_s <= q_pos[:, None, None])[:, :, None, :]
    p_s = masked_softmax(jnp.einsum("tgrd,tgld->tgrl", q, k_s) * scale, m_s)
    o_s = jnp.einsum("tgrl,tgld->tgrd", p_s.astype(v_s.dtype), v_s)
    dist = q_pos[:, None] - w_pos[None, :]
    m_w = ((dist >= 0) & (dist <= WINDOW) & (w_pos[None, :] >= 0))[:, None, None, :]
    p_w = masked_softmax(jnp.einsum("tgrd,lgd->tgrl", q, kw) * scale, m_w)
    o_w = jnp.einsum("tgrl,lgd->tgrd", p_w.astype(vw.dtype), vw)
    return gates[..., 0:1] * o_c + gates[..., 1:2] * o_s + gates[..., 2:3] * o_w


def nsa_prompt(z, q_g, k_g, pe, w1, b1, w2):
    n, s, _ = z.shape
    q, cmp_kv, sel_kv, win_kv, gates = nsa_split(z, q_g, k_g)
    kc = rms_norm(compress(cmp_kv[:, :, 0], pe[0], w1[0], b1[0], w2[0]), k_g[0])
    vc = compress(cmp_kv[:, :, 1], pe[1], w1[1], b1[1], w2[1])
    out = nsa_prompt_attend(q, kc, vc, sel_kv, win_kv, gates)
    pages = lambda a: a.reshape(n, s // PAGE_SIZE, PAGE_SIZE, 2, NSA_KV_HEADS, HEAD_DIM)
    wb = min(WINDOW, s)
    return out, pages(cmp_kv), pages(sel_kv), win_kv[:, s - wb:]


_NSA_SEL_TILE = 512


def _softmax_rows(s, mask):
    sm = jnp.where(mask, s, NEG)
    e = jnp.exp(sm - jnp.max(sm, axis=-1, keepdims=True))
    return jnp.where(mask, e / jnp.sum(e, axis=-1, keepdims=True), 0.0)


def _nsa_prompt_kernel(q_ref, gate_ref, kct_ref, vc_ref, smap_ref, kst_ref, vs_ref, kwt_ref, vw_ref, o_ref,
                       m_scr, l_scr, acc_scr):
    qb = pl.program_id(1)
    tq = q_ref.shape[1]
    q_blk = q_ref[0]
    heads_out = []
    n_cmp = kct_ref.shape[3]
    n_sel = smap_ref.shape[1]
    tk = kst_ref.shape[4]
    n_top = min(N_SEL, n_sel)
    scale = HEAD_DIM ** -0.5
    s0 = qb * tq
    pos = s0 + lax.broadcasted_iota(jnp.int32, (tq, 1), 0)
    pos3 = jnp.concatenate([pos] * NSA_GROUP, axis=0)
    gates = gate_ref[0]
    cend = lax.broadcasted_iota(jnp.int32, (1, n_cmp), 1) * CMP_STRIDE + (CMP_LEN - 1)
    blk = lax.broadcasted_iota(jnp.int32, (1, n_sel), 1)
    cur = pos // SEL_BLOCK
    forced = (blk == 0) | (blk == cur) | (blk == cur - 1)
    valid = blk * SEL_BLOCK <= pos
    n_tiles = (s0 + tq + tk - 1) // tk
    n_wt = WINDOW // tq + 1
    wt0 = jnp.maximum(qb - WINDOW // tq, 0)
    kposw = wt0 * tq + lax.broadcasted_iota(jnp.int32, (1, n_wt * tq), 1)
    dist = pos3 - kposw
    mw = (dist >= 0) & (dist <= WINDOW)

    for g in range(NSA_KV_HEADS):
        cols = slice(g * HEAD_DIM, (g + 1) * HEAD_DIM)
        qg = jnp.concatenate([q_blk[:, (g * NSA_GROUP + r) * HEAD_DIM:(g * NSA_GROUP + r + 1) * HEAD_DIM]
                              for r in range(NSA_GROUP)], axis=0)
        qg_bf = qg.astype(_BF16)
        sc = jnp.dot(qg, kct_ref[0, g], precision=_HI, preferred_element_type=_F32) * scale
        p = _softmax_rows(sc, cend <= pos3)
        oc = jnp.dot(p.astype(_BF16), vc_ref[0, g], preferred_element_type=_F32)
        psum = p[0:tq]
        for r in range(1, NSA_GROUP):
            psum = psum + p[r * tq:(r + 1) * tq]
        imp = jnp.dot(psum, smap_ref[...], precision=_HI, preferred_element_type=_F32)
        imp = jnp.where(valid, imp + jnp.where(forced, FORCE_BONUS, 0.0), NEG)
        _, rank_t = _top_rows(imp.T, n_top)
        sel = (rank_t < float(n_top)).astype(_F32).T.astype(_BF16)
        m_scr[...] = jnp.full(m_scr.shape, NEG, _F32)
        l_scr[...] = jnp.zeros(l_scr.shape, _F32)
        acc_scr[...] = jnp.zeros(acc_scr.shape, _F32)

        def body(kt, carry):
            s = jnp.dot(qg_bf, kst_ref[0, g, kt], preferred_element_type=_F32) * scale
            brow = lax.broadcasted_iota(jnp.int32, (n_sel, tk), 0)
            bkey = kt * (tk // SEL_BLOCK) + lax.broadcasted_iota(jnp.int32, (n_sel, tk), 1) // SEL_BLOCK
            selx = jnp.dot(sel, (brow == bkey).astype(_BF16), preferred_element_type=_F32)
            kpos = kt * tk + lax.broadcasted_iota(jnp.int32, (1, tk), 1)
            mk = jnp.where((selx > 0.5) & (kpos <= pos), 1.0, 0.0)
            mk3 = jnp.concatenate([mk] * NSA_GROUP, axis=0) > 0.5
            sm = jnp.where(mk3, s, NEG)
            m_old = m_scr[...]
            m_new = jnp.maximum(m_old, jnp.max(sm, axis=-1, keepdims=True))
            alpha = jnp.exp(m_old - m_new)
            pe = jnp.where(mk3, jnp.exp(sm - m_new), 0.0)
            l_scr[...] = alpha * l_scr[...] + jnp.sum(pe, axis=-1, keepdims=True)
            acc_scr[...] = alpha * acc_scr[...] + jnp.dot(pe.astype(_BF16), vs_ref[0, kt],
                                                          preferred_element_type=_F32)
            m_scr[...] = m_new
            return carry

        lax.fori_loop(0, n_tiles, body, 0)
        osel = acc_scr[:, cols] / l_scr[...]
        kwt = jnp.concatenate([kwt_ref[0, g, wt0 + i] for i in range(n_wt)], axis=1)
        vw = jnp.concatenate([vw_ref[0, wt0 + i] for i in range(n_wt)], axis=0)
        sw = jnp.dot(qg_bf, kwt, preferred_element_type=_F32) * scale
        pw = _softmax_rows(sw, mw)
        ow = jnp.dot(pw.astype(_BF16), vw, preferred_element_type=_F32)[:, cols]
        for r in range(NSA_GROUP):
            c = (g * NSA_GROUP + r) * 3
            rows = slice(r * tq, (r + 1) * tq)
            heads_out.append(gates[:, c:c + 1] * oc[rows] + gates[:, c + 1:c + 2] * osel[rows]
                             + gates[:, c + 2:c + 3] * ow[rows])
    o_ref[0] = jnp.concatenate(heads_out, axis=1)


def nsa_prompt_attend(q, kc, vc, sel_kv, win_kv, gates):
    n, s = q.shape[:2]
    G, R, HD = NSA_KV_HEADS, NSA_GROUP, HEAD_DIM
    tq, tk = NSA_QBLOCK, _NSA_SEL_TILE
    n_cmp = kc.shape[1]
    ncp = s // CMP_STRIDE
    n_sel = s // SEL_BLOCK
    assert s % tk == 0 and s >= WINDOW + tq
    padc = ((0, 0), (0, ncp - n_cmp), (0, 0), (0, 0))
    qh = q.reshape(n, s, MIX_W)
    kct = jnp.pad(kc, padc).transpose(0, 2, 3, 1)
    vcb = jnp.pad(vc, padc).transpose(0, 2, 1, 3).astype(_BF16)
    smap = jnp.pad(selection_map(n_cmp, n_sel), ((0, ncp - n_cmp), (0, 0)))
    kst = sel_kv[:, :, 0].astype(_BF16).reshape(n, s // tk, tk, G, HD).transpose(0, 3, 1, 4, 2)
    vs = sel_kv[:, :, 1].astype(_BF16).reshape(n, s // tk, tk, G * HD)
    kwt = win_kv[:, :, 0].astype(_BF16).reshape(n, s // tq, tq, G, HD).transpose(0, 3, 1, 4, 2)
    vw = win_kv[:, :, 1].astype(_BF16).reshape(n, s // tq, tq, G * HD)
    once = pl.Buffered(1)
    out = pl.pallas_call(
        _nsa_prompt_kernel,
        grid=(n, s // tq),
        in_specs=[
            pl.BlockSpec((1, tq, MIX_W), lambda b, i: (b, i, 0)),
            pl.BlockSpec((1, tq, G * R * 3), lambda b, i: (b, i, 0)),
            pl.BlockSpec((1, G, HD, ncp), lambda b, i: (b, 0, 0, 0)),
            pl.BlockSpec((1, G, ncp, HD), lambda b, i: (b, 0, 0, 0)),
            pl.BlockSpec((ncp, n_sel), lambda b, i: (0, 0)),
            pl.BlockSpec((1, G, s // tk, HD, tk), lambda b, i: (b, 0, 0, 0, 0), pipeline_mode=once),
            pl.BlockSpec((1, s // tk, tk, G * HD), lambda b, i: (b, 0, 0, 0), pipeline_mode=once),
            pl.BlockSpec((1, G, s // tq, HD, tq), lambda b, i: (b, 0, 0, 0, 0), pipeline_mode=once),
            pl.BlockSpec((1, s // tq, tq, G * HD), lambda b, i: (b, 0, 0, 0), pipeline_mode=once),
        ],
        out_specs=pl.BlockSpec((1, tq, MIX_W), lambda b, i: (b, i, 0)),
        out_shape=jax.ShapeDtypeStruct((n, s, MIX_W), _F32),
        scratch_shapes=[pltpu.VMEM((R * tq, 1), _F32), pltpu.VMEM((R * tq, 1), _F32),
                        pltpu.VMEM((R * tq, G * HD), _F32)],
        compiler_params=pltpu.CompilerParams(dimension_semantics=("arbitrary", "arbitrary"),
                                             vmem_limit_bytes=VMEM_BYTES_V7X * 3 // 4),
        name="nsa_prompt",
    )(qh, gates.reshape(n, s, G * R * 3), kct, vcb, smap, kst, vs, kwt, vw)
    return out


_BF16_ROWS = 16


def _softmax_cols_aug(sT, mask, v_aug):
    sm = jnp.where(mask, sT, NEG)
    e = jnp.where(mask, jnp.exp(sm - jnp.max(sm, axis=0, keepdims=True)), 0.0)
    acc = lax.dot_general(e.astype(_BF16), v_aug, (((0,), (0,)), ((), ())), preferred_element_type=_F32)
    c = v_aug.shape[1] - LANES
    return acc[:, :c] / acc[:, c:c + 1]


def _nsa_sample_kernel(pt_ref, qbd_ref, kc_ref, vc_ref, smapt_ref, rsum_ref, rexp_ref, gate_ref, *rest,
                       n_pages, past, t_new, n_cmp, n_sel, wb):
    page_refs = rest[:n_pages]
    selnew_ref, win_ref, o_ref, s_scr = rest[n_pages:]
    kw = NSA_KV_HEADS * HEAD_DIM
    scale = HEAD_DIM ** -0.5
    n_top = min(N_SEL, n_sel)
    qbd = qbd_ref[0]
    qbd_bf = qbd.astype(_BF16)
    nl = qbd.shape[1]
    pos = past + lax.broadcasted_iota(jnp.int32, (1, nl), 1) % t_new
    ones_cols = lambda k: jnp.ones((k, LANES), _BF16)

    ncp = kc_ref.shape[1]
    ci = lax.broadcasted_iota(jnp.int32, (ncp, 1), 0)
    mc = (ci * CMP_STRIDE + (CMP_LEN - 1) <= pos) & (ci < n_cmp)
    sc = jnp.dot(kc_ref[0], qbd, precision=_HI, preferred_element_type=_F32) * scale
    scm = jnp.where(mc, sc, NEG)
    e = jnp.exp(scm - jnp.max(scm, axis=0, keepdims=True))
    pc = jnp.where(mc, e / jnp.sum(e, axis=0, keepdims=True), 0.0)
    oc = lax.dot_general(pc.astype(_BF16), vc_ref[0], (((0,), (0,)), ((), ())), preferred_element_type=_F32)
    psum = jnp.dot(pc, rsum_ref[...], precision=_HI, preferred_element_type=_F32)
    imp = jnp.dot(smapt_ref[...], psum, precision=_HI, preferred_element_type=_F32)
    nb, nq = imp.shape
    pos_q = past + lax.broadcasted_iota(jnp.int32, (1, nq), 1) % t_new
    blk = lax.broadcasted_iota(jnp.int32, (nb, 1), 0)
    cur = pos_q // SEL_BLOCK
    forced = (blk == 0) | (blk == cur) | (blk == cur - 1)
    imp = jnp.where(blk * SEL_BLOCK <= pos_q, imp + jnp.where(forced, FORCE_BONUS, 0.0), NEG)
    imp = jnp.where(blk < n_sel, imp, -jnp.inf)
    _, rank = _top_rows(imp, n_top)
    sel = jnp.dot((rank < float(n_top)).astype(_BF16), rexp_ref[...], preferred_element_type=_F32)

    rowi = lax.broadcasted_iota(jnp.int32, (PAGE_SIZE, 1), 0)
    per_page = PAGE_SIZE // SEL_BLOCK
    for p in range(n_pages):
        kp = page_refs[p][0, :, 0:kw].astype(_BF16)
        s = jnp.dot(kp, qbd_bf, preferred_element_type=_F32) * scale
        pick = sel[p * per_page:p * per_page + 1, :]
        for b in range(1, per_page):
            pick = jnp.where(rowi >= b * SEL_BLOCK, sel[p * per_page + b:p * per_page + b + 1, :], pick)
        ok = (pick > 0.5) & (p * PAGE_SIZE + rowi <= pos)
        s_scr[p * PAGE_SIZE:(p + 1) * PAGE_SIZE, :] = jnp.where(ok, s, NEG)
    n_new = selnew_ref.shape[1]
    rown = lax.broadcasted_iota(jnp.int32, (n_new, 1), 0)
    kn = selnew_ref[0, :, 0:kw].astype(_BF16).astype(_F32)
    s = jnp.dot(kn, qbd_bf.astype(_F32), precision=_HI, preferred_element_type=_F32) * scale
    ok = (sel[past // SEL_BLOCK:past // SEL_BLOCK + 1, :] > 0.5) & (past + rown <= pos)
    s_scr[past:past + n_new, :] = jnp.where(ok, s, NEG)
    mx = jnp.max(s_scr[...], axis=0, keepdims=True)
    acc = jnp.zeros((nl, kw + LANES), _F32)
    for p in range(n_pages):
        sp = s_scr[p * PAGE_SIZE:(p + 1) * PAGE_SIZE, :]
        ep = jnp.where(sp > 0.5 * NEG, jnp.exp(sp - mx), 0.0).astype(_BF16)
        v_aug = jnp.concatenate([page_refs[p][0, :, kw:2 * kw].astype(_BF16), ones_cols(PAGE_SIZE)], axis=1)
        acc = acc + lax.dot_general(ep, v_aug, (((0,), (0,)), ((), ())), preferred_element_type=_F32)
    sp = s_scr[past:past + n_new, :]
    ep = jnp.where(sp > 0.5 * NEG, jnp.exp(sp - mx), 0.0).astype(_BF16)
    v_aug = jnp.concatenate([selnew_ref[0, :, kw:2 * kw].astype(_BF16), ones_cols(n_new)], axis=1)
    acc = acc + lax.dot_general(ep, v_aug, (((0,), (0,)), ((), ())), preferred_element_type=_F32)
    osel = acc[:, :kw] / acc[:, kw:kw + 1]

    n_w = win_ref.shape[1]
    wpos = past - wb + lax.broadcasted_iota(jnp.int32, (n_w, 1), 0)
    dist = pos - wpos
    mw = (dist >= 0) & (dist <= WINDOW) & (wpos >= 0)
    sw = jnp.dot(win_ref[0, :, 0:kw].astype(_BF16), qbd_bf, preferred_element_type=_F32) * scale
    v_aug = jnp.concatenate([win_ref[0, :, kw:2 * kw].astype(_BF16), ones_cols(n_w)], axis=1)
    ow = _softmax_cols_aug(sw, mw, v_aug)

    g = gate_ref[0]
    o_ref[0] = g[:, 0:1] * oc + g[:, 1:2] * osel + g[:, 2:3] * ow


def nsa_sample_attend(q, kc, vc, pool_sel, page_table, sel_new, win_all, gates, wb):
    n, t_new = q.shape[:2]
    G, R, HD = NSA_KV_HEADS, NSA_GROUP, HEAD_DIM
    n_pages = page_table.shape[1]
    past = n_pages * PAGE_SIZE
    n_cmp = kc.shape[1]
    n_sel = -(-(past + t_new) // SEL_BLOCK)
    assert past % SEL_BLOCK == 0 and t_new <= SEL_BLOCK and PAGE_SIZE % SEL_BLOCK == 0
    kw = G * HD
    nl = G * R * t_new
    up = lambda v, m: -(-v // m) * m
    ncp, nbp = up(n_cmp, LANES), up(n_sel, 8)
    n_new, n_w = up(t_new, _BF16_ROWS), up(wb + t_new, _BF16_ROWS)
    qbd = jnp.einsum("ntgrd,gh->ngdhrt", q, jnp.eye(G, dtype=q.dtype)).reshape(n, kw, nl)
    kcp = jnp.pad(kc.reshape(n, n_cmp, kw), ((0, 0), (0, ncp - n_cmp), (0, 0)))
    vcp = jnp.pad(vc.reshape(n, n_cmp, kw), ((0, 0), (0, ncp - n_cmp), (0, 0))).astype(_BF16)
    smapt = jnp.pad(selection_map(n_cmp, n_sel).T, ((0, nbp - n_sel), (0, ncp - n_cmp)))
    rsum = np.zeros((G, R, t_new, G, t_new), np.float32)
    for g in range(G):
        for t in range(t_new):
            rsum[g, :, t, g, t] = 1.0
    rsum = rsum.reshape(nl, G * t_new)
    gates3 = gates.transpose(0, 2, 3, 1, 4).reshape(n, nl, 3)
    pages = pool_sel.reshape(pool_sel.shape[0], PAGE_SIZE, 2 * kw)
    seln = jnp.pad(sel_new.reshape(n, t_new, 2 * kw), ((0, 0), (0, n_new - t_new), (0, 0)))
    win = jnp.pad(win_all.reshape(n, wb + t_new, 2 * kw), ((0, 0), (0, n_w - wb - t_new), (0, 0)))
    whole = lambda *shape: pl.BlockSpec(shape, lambda b, pt: (0,) * len(shape))
    per_seq = lambda *shape: pl.BlockSpec((1,) + shape, lambda b, pt: (b,) + (0,) * len(shape))
    page_spec = lambda p: pl.BlockSpec((1, PAGE_SIZE, 2 * kw), lambda b, pt: (pt[b, p], 0, 0))
    out = pl.pallas_call(
        functools.partial(_nsa_sample_kernel, n_pages=n_pages, past=past, t_new=t_new, n_cmp=n_cmp, n_sel=n_sel,
                          wb=wb),
        grid_spec=pltpu.PrefetchScalarGridSpec(
            num_scalar_prefetch=1,
            grid=(n,),
            in_specs=[per_seq(kw, nl), per_seq(ncp, kw), per_seq(ncp, kw), whole(nbp, ncp),
                      whole(nl, G * t_new), whole(G * t_new, nl), per_seq(nl, 3)]
                     + [page_spec(p) for p in range(n_pages)]
                     + [per_seq(n_new, 2 * kw), per_seq(n_w, 2 * kw)],
            out_specs=per_seq(nl, kw),
            scratch_shapes=[pltpu.VMEM((past + n_new, nl), _F32)],
        ),
        out_shape=jax.ShapeDtypeStruct((n, nl, kw), _F32),
        compiler_params=pltpu.CompilerParams(dimension_semantics=("arbitrary",),
                                             vmem_limit_bytes=VMEM_BYTES_V7X // 2),
        name="nsa_sample",
    )(page_table, qbd, kcp, vcp, smapt, jnp.asarray(rsum), jnp.asarray(rsum.T).astype(_BF16), gates3,
      *([pages] * n_pages), seln, win)
    o = out.reshape(n, G, R, t_new, G, HD)
    o = jnp.stack([o[:, g, :, :, g, :] for g in range(G)], axis=1)
    return o.transpose(0, 3, 1, 2, 4).reshape(n, t_new, MIX_W)


def nsa_sample(z, pool_cmp, pool_sel, win_buf, page_table, q_g, k_g, pe, w1, b1, w2):
    n, t_new, _ = z.shape
    past = page_table.shape[1] * PAGE_SIZE
    G, HD = NSA_KV_HEADS, HEAD_DIM
    q, cmp_new, sel_new, win_new, gates = nsa_split(z, q_g, k_g)
    win_all = jnp.concatenate([win_buf, win_new], axis=1)
    wb = win_buf.shape[1]
    n_ch = (past + t_new) // CMP_STRIDE
    assert PAGE_SIZE % CMP_STRIDE == 0 and n_ch * CMP_STRIDE <= past
    per_page = PAGE_SIZE // CMP_STRIDE
    pool6 = pool_cmp.reshape(pool_cmp.shape[0], per_page, CMP_STRIDE, 2, G, HD)
    w1r = w1.reshape(2, 2, CMP_STRIDE, HD, CMP_HIDDEN)
    a_pool = jnp.einsum("pcskgd,kjsdh->pckjgh", pool6, w1r)
    a_seq = a_pool[page_table].reshape(n, past // CMP_STRIDE, 2, 2, G, CMP_HIDDEN)[:, :n_ch]
    bias = jnp.einsum("kx,kxh->kh", pe.reshape(2, -1), w1) + b1
    pre = a_seq[:, :-1, :, 0] + a_seq[:, 1:, :, 1] + bias[None, None, :, None, :]
    comp = jnp.einsum("nckgh,khd->nckgd", jax.nn.gelu(pre, approximate=False), w2)
    kc = rms_norm(comp[:, :, 0], k_g[0])
    vc = comp[:, :, 1]
    out = nsa_sample_attend(q, kc, vc, pool_sel, page_table, sel_new, win_all, gates, wb)
    return out, cmp_new, sel_new, win_all[:, -wb:]


def shortconv_mix(z, ctx, w, b):
    bg = z[..., :MIX_W]
    cg = z[..., MIX_W:2 * MIX_W]
    v = z[..., 2 * MIX_W:3 * MIX_W]
    u_ext = jnp.concatenate([ctx, cg * v], axis=1)
    y = bg * causal_dwconv(u_ext, w, b)
    return y, u_ext[:, -(CONV_B_WIDTH - 1):]


def conformer_mix(z, ctx, w, b, ln_g, ln_b):
    u = z[..., :MIX_W] * jax.nn.sigmoid(z[..., MIX_W:2 * MIX_W])
    u_ext = jnp.concatenate([ctx, u], axis=1)
    y = jax.nn.silu(layer_norm(causal_dwconv(u_ext, w, b), ln_g, ln_b))
    return y, u_ext[:, -(CONV_C_WIDTH - 1):]


_CAND = [(j1, j2) for j1 in range(PEER_TOPK) for j2 in range(PEER_TOPK) if (j1 + 1) * (j2 + 1) <= PEER_TOPK]
_N_CAND = len(_CAND)
_CAND_ROWS = -(-_N_CAND // 8) * 8
_PEER_SEL_TILE = 256
_PEER_TOK_TILE = 512
_PEER_EXP_BLOCK = 1024
_PEER_SUB_BLOCK = 256
_PEER_ROW_GROUP = 32


def _cand_tables():
    j1 = np.full((_CAND_ROWS, LANES), -1.0, np.float32)
    j2 = np.full((_CAND_ROWS, LANES), -1.0, np.float32)
    flat = np.full((_CAND_ROWS, LANES), 1.0e6, np.float32)
    for r, (a, b) in enumerate(_CAND):
        j1[r], j2[r], flat[r] = a, b, a * PEER_TOPK + b
    return np.stack([j1, j2, flat])


def _gelu_erf(x):
    return 0.5 * x * (1.0 + lax.erf(x * (2.0 ** -0.5)))


def _top_rows(s, n_top):
    (vals, rank), = _top_rows_many([s], n_top)
    return vals, rank


def _top_rows_many(arrays, n_top):
    k, l = arrays[0].shape
    kidx = lax.broadcasted_iota(jnp.int32, (k, l), 0).astype(_F32)
    jrow = lax.broadcasted_iota(jnp.int32, (n_top, l), 0)
    work = list(arrays)
    rank = [jnp.full((k, l), float(n_top), _F32) for _ in arrays]
    vals = [jnp.zeros((n_top, l), _F32) for _ in arrays]
    for j in range(n_top):
        for i in range(len(arrays)):
            m = jnp.max(work[i], axis=0, keepdims=True)
            first = jnp.min(jnp.where(work[i] == m, kidx, float(k)), axis=0, keepdims=True)
            sel = kidx == first
            rank[i] = jnp.where(sel, float(j), rank[i])
            work[i] = jnp.where(sel, -jnp.inf, work[i])
            vals[i] = jnp.where(jrow == j, m, vals[i])
    return list(zip(vals, rank))


def _peer_select_kernel(x_ref, g_ref, wq_hi_ref, wq_lo_ref, sk_ref, ct_ref, ht_ref, p1_ref, c1_ref, p2_ref, r2_ref,
                        h_hi_scr, h_lo_scr):
    hd = pl.program_id(1)

    @pl.when(hd == 0)
    def _():
        x = x_ref[...]
        h = x * lax.rsqrt(jnp.mean(x * x, axis=-1, keepdims=True) + EPS) * g_ref[...]
        h_hi = h.astype(_BF16)
        h_hi_scr[...] = h_hi
        h_lo_scr[...] = (h - h_hi.astype(_F32)).astype(_BF16)
        ht_ref[...] = h.T.astype(_BF16)

    h_hi, h_lo = h_hi_scr[...], h_lo_scr[...]
    q = (jnp.dot(h_hi, wq_hi_ref[...], preferred_element_type=_F32)
         + jnp.dot(h_lo, wq_hi_ref[...], preferred_element_type=_F32)
         + jnp.dot(h_hi, wq_lo_ref[...], preferred_element_type=_F32))
    nt = (((1,), (1,)), ((), ()))
    half = PEER_DK // 2
    s1_all = lax.dot_general(sk_ref[0, 0], q[:, :half], nt, precision=_HI, preferred_element_type=_F32)
    s2_all = lax.dot_general(sk_ref[0, 1], q[:, half:], nt, precision=_HI, preferred_element_type=_F32)
    cj1, cj2, cflat = ct_ref[0], ct_ref[1], ct_ref[2]
    for c in range(x_ref.shape[0] // LANES):
        sl = slice(c * LANES, (c + 1) * LANES)
        s1, s2 = s1_all[:, sl], s2_all[:, sl]
        (v1, rank1), (v2, rank2) = _top_rows_many([s1, s2], PEER_TOPK)
        a1 = jnp.zeros((_CAND_ROWS, LANES), _F32)
        a2 = jnp.zeros((_CAND_ROWS, LANES), _F32)
        for j in range(PEER_TOPK):
            a1 = jnp.where(cj1 == float(j), v1[j:j + 1, :], a1)
            a2 = jnp.where(cj2 == float(j), v2[j:j + 1, :], a2)
        cand = jnp.where(cj1 >= 0.0, a1 + a2, -jnp.inf)
        work = cand
        picked = jnp.zeros((_CAND_ROWS, LANES), _F32)
        for _ in range(PEER_TOPK):
            m = jnp.max(work, axis=0, keepdims=True)
            first = jnp.min(jnp.where(work == m, cflat, 2.0e6), axis=0, keepdims=True)
            sel = cflat == first
            picked = jnp.where(sel, 1.0, picked)
            work = jnp.where(sel, -jnp.inf, work)
        top = v1[0:1, :] + v2[0:1, :]
        z = jnp.sum(jnp.where(picked > 0.0, jnp.exp(cand - top), 0.0), axis=0, keepdims=True)
        c1 = jnp.zeros((PEER_KEYS, LANES), _F32)
        for j in range(PEER_TOPK):
            cnt_j = jnp.sum(jnp.where(cj1 == float(j), picked, 0.0), axis=0, keepdims=True)
            c1 = jnp.where(rank1 == float(j), cnt_j, c1)
        p1_ref[0, :, sl] = jnp.exp(s1 - v1[0:1, :]) / z
        c1_ref[0, :, sl] = c1
        p2_ref[0, :, sl] = jnp.exp(s2 - v2[0:1, :])
        r2_ref[0, :, sl] = rank2


def _peer_select(x, g, w_q, sub_keys):
    n, d = x.shape
    ts = _PEER_SEL_TILE
    assert n % ts == 0
    w_hi = w_q.astype(_BF16)
    w_lo = (w_q - w_hi.astype(_F32)).astype(_BF16)
    fac = jax.ShapeDtypeStruct((PEER_HEADS, PEER_KEYS, n), _F32)
    fac_spec = pl.BlockSpec((1, PEER_KEYS, ts), lambda i, h: (h, 0, i))
    return pl.pallas_call(
        _peer_select_kernel,
        grid=(n // ts, PEER_HEADS),
        in_specs=[
            pl.BlockSpec((ts, d), lambda i, h: (i, 0)),
            pl.BlockSpec((1, d), lambda i, h: (0, 0)),
            pl.BlockSpec((d, PEER_DK), lambda i, h: (0, h)),
            pl.BlockSpec((d, PEER_DK), lambda i, h: (0, h)),
            pl.BlockSpec((1, 2, PEER_KEYS, PEER_DK // 2), lambda i, h: (h, 0, 0, 0)),
            pl.BlockSpec((3, _CAND_ROWS, LANES), lambda i, h: (0, 0, 0)),
        ],
        out_specs=[pl.BlockSpec((d, ts), lambda i, h: (0, i)), fac_spec, fac_spec, fac_spec, fac_spec],
        out_shape=[jax.ShapeDtypeStruct((d, n), _BF16), fac, fac, fac, fac],
        scratch_shapes=[pltpu.VMEM((ts, d), _BF16), pltpu.VMEM((ts, d), _BF16)],
        compiler_params=pltpu.CompilerParams(dimension_semantics=("arbitrary", "arbitrary"),
                                             vmem_limit_bytes=VMEM_BYTES_V7X * 3 // 4),
        name="peer_select",
    )(x, g.reshape(1, d), w_hi, w_lo, sub_keys, jnp.asarray(_cand_tables()))


def _peer_dense_kernel(x_ref, ht_ref, u_ref, vt_ref, p1_ref, c1_ref, p2_ref, r2_ref, o_ref,
                       acc_ref, *scratch):
    j = pl.program_id(1)

    @pl.when(j == 0)
    def _():
        acc_ref[...] = jnp.zeros_like(acc_ref)

    eb, tt = u_ref.shape[0], ht_ref.shape[1]
    sub = _PEER_SUB_BLOCK
    n_sub = eb // sub
    st_refs, wact_refs = scratch[:2], scratch[2:]
    half = tt // 2
    d = vt_ref.shape[0]
    n_out = d // sub

    def scores(k, nh):
        lanes = slice(nh * half, (nh + 1) * half)
        st_refs[k % 2][:, lanes] = jnp.dot(u_ref[k * sub:(k + 1) * sub, :], ht_ref[:, lanes],
                                           preferred_element_type=_F32)

    def output_piece(k, q):
        r = slice(q * sub, (q + 1) * sub)
        acc_ref[r, :] += jnp.dot(vt_ref[r, k * sub:(k + 1) * sub], wact_refs[k % 2][...],
                                 preferred_element_type=_F32)

    def build_chunk(k, al, c, a):
        lanes = slice(c * LANES, (c + 1) * LANES)
        c1 = [c1_ref[h, pl.ds(a, 1), :][:, lanes] for h in range(PEER_HEADS)]
        p1 = [p1_ref[h, pl.ds(a, 1), :][:, lanes] for h in range(PEER_HEADS)]
        for rg in range(PEER_KEYS // _PEER_ROW_GROUP):
            b = slice(rg * _PEER_ROW_GROUP, (rg + 1) * _PEER_ROW_GROUP)
            brows = slice(al * PEER_KEYS + b.start, al * PEER_KEYS + b.stop)
            w = jnp.zeros((_PEER_ROW_GROUP, LANES), _F32)
            for h in range(PEER_HEADS):
                w = w + jnp.where(r2_ref[h, b, lanes] < c1[h], p2_ref[h, b, lanes], 0.0) * p1[h]
            wact_refs[k % 2][brows, lanes] = (w * _gelu_erf(st_refs[k % 2][brows, lanes])).astype(_BF16)

    n_al, n_c = sub // PEER_KEYS, tt // LANES
    n_chunks = n_al * n_c
    scores(0, 0)
    scores(0, 1)
    for k in range(n_sub):
        i = 0
        for al in range(n_al):
            a = j * (eb // PEER_KEYS) + k * n_al + al
            for c in range(n_c):
                build_chunk(k, al, c, a)
                if k + 1 < n_sub and i % (n_chunks // 2) == 0:
                    scores(k + 1, i // (n_chunks // 2))
                if k >= 1 and i % (n_chunks // n_out) == n_chunks // n_out - 1:
                    output_piece(k - 1, i // (n_chunks // n_out))
                i += 1
    for q in range(n_out):
        output_piece(n_sub - 1, q)

    @pl.when(j == pl.num_programs(1) - 1)
    def _():
        o_ref[...] = x_ref[...] + acc_ref[...].T


def _peer_dense(x, ht, u_bf, vt_bf, p1, c1, p2, r2):
    n, d = x.shape
    tt, eb = _PEER_TOK_TILE, _PEER_EXP_BLOCK
    assert n % tt == 0 and N_EXPERTS % eb == 0
    fac_spec = pl.BlockSpec((PEER_HEADS, PEER_KEYS, tt), lambda i, j: (0, 0, i))
    return pl.pallas_call(
        _peer_dense_kernel,
        grid=(n // tt, N_EXPERTS // eb),
        in_specs=[
            pl.BlockSpec((tt, d), lambda i, j: (i, 0)),
            pl.BlockSpec((d, tt), lambda i, j: (0, i)),
            pl.BlockSpec((eb, d), lambda i, j: (j, 0)),
            pl.BlockSpec((d, eb), lambda i, j: (0, j)),
            fac_spec, fac_spec, fac_spec, fac_spec,
        ],
        out_specs=pl.BlockSpec((tt, d), lambda i, j: (i, 0)),
        out_shape=jax.ShapeDtypeStruct((n, d), _F32),
        scratch_shapes=([pltpu.VMEM((d, tt), _F32)]
                        + [pltpu.VMEM((_PEER_SUB_BLOCK, tt), _F32)] * 2
                        + [pltpu.VMEM((_PEER_SUB_BLOCK, tt), _BF16)] * 2),
        compiler_params=pltpu.CompilerParams(dimension_semantics=("arbitrary", "arbitrary"),
                                             vmem_limit_bytes=VMEM_BYTES_V7X * 3 // 4),
        name="peer_dense",
    )(x, ht, u_bf, vt_bf, p1, c1, p2, r2)


def peer_residual(x, g, w_q, sub_keys, u_tab, v_tab):
    ht, p1, c1, p2, r2 = _peer_select(x, g, w_q, sub_keys)
    return _peer_dense(x, ht, u_tab.astype(_BF16), v_tab.T.astype(_BF16), p1, c1, p2, r2)


def kernel(x_prompt, x_sample, cache_a_cmp_kv, cache_a_sel_kv, cache_a_win_kv, state_b_conv, state_c_conv,
           cache_mem_kv, page_table, mem_prompt, norm_mix_g, norm_mem_g, w_mem_kv, mem_q_norm_g, mem_k_norm_g,
           w_out, norm_ffn_g, peer_w_q, peer_sub_keys, peer_u, peer_v, a_w_in, a_q_norm_g, a_k_norm_g,
           a_cmp_pe, a_cmp_w1, a_cmp_b1, a_cmp_w2, b_w_in, b_conv_w, b_conv_b, c_w_in, c_conv_w, c_conv_b,
           c_ln_g, c_ln_b):
    (n_p, s_len), (n_s, t_new) = x_prompt.shape[:2], x_sample.shape[:2]
    n_tok_p = n_p * s_len
    x_all = jnp.concatenate([x_prompt.reshape(-1, D_MODEL), x_sample.reshape(-1, D_MODEL)], axis=0)
    p_cmp, p_sel, p_win, p_cb, p_cc, p_mem = [], [], [], [], [], []
    s_cmp, s_sel, s_win, s_cb, s_cc = [], [], [], [], []
    for i in range(DEPTH):
        kind, li = i % N_MIXERS, i // N_MIXERS
        w_in = (a_w_in, b_w_in, c_w_in)[kind][li]
        z_all = norm_proj(x_all, norm_mix_g[i], w_in)
        zp = z_all[:n_tok_p].reshape(n_p, s_len, -1)
        zs = z_all[n_tok_p:].reshape(n_s, t_new, -1)
        mem_cols = slice(w_in.shape[1] - MEM_W, w_in.shape[1])
        if kind == 0:
            nsa_w = (a_q_norm_g[li], a_k_norm_g[li], a_cmp_pe[li], a_cmp_w1[li], a_cmp_b1[li], a_cmp_w2[li])
            mp, c_p, sl_p, w_p = nsa_prompt(zp, *nsa_w)
            ms, c_s, sl_s, w_s = nsa_sample(zs, cache_a_cmp_kv[li], cache_a_sel_kv[li], cache_a_win_kv[li],
                                            page_table, *nsa_w)
            p_cmp.append(c_p)
            p_sel.append(sl_p)
            p_win.append(w_p)
            s_cmp.append(c_s)
            s_sel.append(sl_s)
            s_win.append(w_s)
        elif kind == 1:
            mp, st_p = shortconv_mix(zp, jnp.zeros((n_p, CONV_B_WIDTH - 1, MIX_W), zp.dtype), b_conv_w[li], b_conv_b[li])
            ms, st_s = shortconv_mix(zs, state_b_conv[li], b_conv_w[li], b_conv_b[li])
            p_cb.append(st_p)
            s_cb.append(st_s)
        else:
            mp, st_p = conformer_mix(zp, jnp.zeros((n_p, CONV_C_WIDTH - 1, MIX_W), zp.dtype),
                                     c_conv_w[li], c_conv_b[li], c_ln_g[li], c_ln_b[li])
            ms, st_s = conformer_mix(zs, state_c_conv[li], c_conv_w[li], c_conv_b[li], c_ln_g[li], c_ln_b[li])
            p_cc.append(st_p)
            s_cc.append(st_s)
        mkv_p = memory_kv(mem_prompt, norm_mem_g[i], w_mem_kv[i], mem_k_norm_g[i])
        p_mem.append(mkv_p)
        op = jnp.concatenate([mp, memory_attend(zp[..., mem_cols], mkv_p, mem_q_norm_g[i])], axis=-1)
        os_ = jnp.concatenate([ms, memory_attend(zs[..., mem_cols], cache_mem_kv[i], mem_q_norm_g[i])], axis=-1)
        o_all = jnp.concatenate([op.reshape(-1, D_MODEL), os_.reshape(-1, D_MODEL)], axis=0)
        x_all = out_proj_residual(o_all, w_out[i], x_all)
        x_all = peer_residual(x_all, norm_ffn_g[i], peer_w_q[i], peer_sub_keys[i], peer_u[i], peer_v[i])
    xp = x_all[:n_tok_p].reshape(x_prompt.shape)
    xs = x_all[n_tok_p:].reshape(x_sample.shape)
    return (xp, xs, jnp.stack(p_cmp), jnp.stack(p_sel), jnp.stack(p_win), jnp.stack(p_cb), jnp.stack(p_cc),
            jnp.stack(p_mem), jnp.stack(s_cmp), jnp.stack(s_sel), jnp.stack(s_win), jnp.stack(s_cb), jnp.stack(s_cc))
```

```python
import functools

import numpy as np
import jax
import jax.numpy as jnp
from jax import lax
from jax.experimental import pallas as pl
from jax.experimental.pallas import tpu as pltpu

D_MODEL = 1024
BATCH = 2
SEQ = 8192
DEPTH = 4
DEC_BATCH = 128
DEC_SEQ = 8
PAST_LEN = 2048
PAGE_SIZE = 128

N_MIXERS = 3
HEAD_DIM = 64
MIX_W = 3 * D_MODEL // 4
MEM_W = D_MODEL - MIX_W
MEM_HEADS = MEM_W // HEAD_DIM
N_MEM = 256
NSA_HEADS = MIX_W // HEAD_DIM
NSA_KV_HEADS = 4
NSA_GROUP = NSA_HEADS // NSA_KV_HEADS
NSA_KV_W = 2 * NSA_KV_HEADS * HEAD_DIM
CMP_STRIDE = 16
CMP_LEN = 2 * CMP_STRIDE
CMP_HIDDEN = 64
SEL_BLOCK = 64
N_SEL = 16
WINDOW = 512
NSA_QBLOCK = 128
FORCE_BONUS = 1.0e4
A_IN_W = MIX_W + 3 * NSA_KV_W + 3 * NSA_HEADS + MEM_W
CONV_B_WIDTH = 3
B_IN_W = 3 * MIX_W + MEM_W
CONV_C_WIDTH = 31
C_IN_W = 2 * MIX_W + MEM_W
PEER_HEADS = 8
PEER_KEYS = 128
N_EXPERTS = PEER_KEYS * PEER_KEYS
PEER_TOPK = 16
PEER_DK = 256
EPS = 1e-6
NEG = -1e30

LANES = 128
VMEM_BYTES_V7X = 64 * 1024 * 1024

_F32 = jnp.float32
_BF16 = jnp.bfloat16
_HI = lax.Precision.HIGHEST


def rms_norm(x, g):
    xf = x.astype(jnp.float32)
    y = xf * lax.rsqrt(jnp.mean(xf * xf, axis=-1, keepdims=True) + EPS)
    return (y * g.astype(jnp.float32)).astype(x.dtype)


def layer_norm(x, g, b):
    xf = x.astype(jnp.float32)
    mu = jnp.mean(xf, axis=-1, keepdims=True)
    var = jnp.mean(jnp.square(xf - mu), axis=-1, keepdims=True)
    y = (xf - mu) * lax.rsqrt(var + EPS)
    return (y * g.astype(jnp.float32) + b.astype(jnp.float32)).astype(x.dtype)


def masked_softmax(s, mask):
    p = jax.nn.softmax(jnp.where(mask, s.astype(jnp.float32), NEG), axis=-1)
    return jnp.where(mask, p, 0.0)


def causal_dwconv(u_ext, w, b):
    y = lax.conv_general_dilated(u_ext, w[:, None, :].astype(u_ext.dtype), window_strides=(1,),
                                 padding="VALID", dimension_numbers=("NWC", "WIO", "NWC"),
                                 feature_group_count=u_ext.shape[-1])
    return y + b


_PROJ_ROWS = 512
_PROJ_MAX_COLS = 1024


def _norm_proj_kernel(x_ref, g_ref, w_ref, o_ref, h_scr):
    @pl.when(pl.program_id(1) == 0)
    def _():
        x = x_ref[...]
        h = x * lax.rsqrt(jnp.mean(x * x, axis=-1, keepdims=True) + EPS) * g_ref[...]
        h_scr[...] = h.astype(_BF16)

    o_ref[...] = jnp.dot(h_scr[...], w_ref[...], preferred_element_type=_F32)


def norm_proj(x, g, w):
    n, d = x.shape
    m = w.shape[1]
    groups = -(-m // LANES)
    per_tile = max(k for k in range(1, _PROJ_MAX_COLS // LANES + 1) if groups % k == 0)
    tn, mp = per_tile * LANES, groups * LANES
    rows = min(_PROJ_ROWS, n)
    assert n % rows == 0
    wp = jnp.pad(w, ((0, 0), (0, mp - m))).astype(_BF16)
    return pl.pallas_call(
        _norm_proj_kernel,
        grid=(n // rows, mp // tn),
        in_specs=[pl.BlockSpec((rows, d), lambda i, j: (i, 0)),
                  pl.BlockSpec((1, d), lambda i, j: (0, 0)),
                  pl.BlockSpec((d, tn), lambda i, j: (0, j))],
        out_specs=pl.BlockSpec((rows, tn), lambda i, j: (i, j)),
        out_shape=jax.ShapeDtypeStruct((n, mp), _F32),
        scratch_shapes=[pltpu.VMEM((rows, d), _BF16)],
        compiler_params=pltpu.CompilerParams(dimension_semantics=("arbitrary", "arbitrary")),
        name="norm_proj",
    )(x, g.reshape(1, d), wp)


def _out_proj_kernel(o_ref, w_ref, x_ref, y_ref):
    y_ref[...] = x_ref[...] + jnp.dot(o_ref[...].astype(_BF16), w_ref[...], preferred_element_type=_F32)


def out_proj_residual(o, w, x):
    n, d = x.shape
    rows = min(_PROJ_ROWS, n)
    assert n % rows == 0 and w.shape == (d, d)
    row_spec = pl.BlockSpec((rows, d), lambda i: (i, 0))
    return pl.pallas_call(
        _out_proj_kernel,
        grid=(n // rows,),
        in_specs=[row_spec, pl.BlockSpec((d, d), lambda i: (0, 0)), row_spec],
        out_specs=row_spec,
        out_shape=jax.ShapeDtypeStruct((n, d), _F32),
        compiler_params=pltpu.CompilerParams(dimension_semantics=("arbitrary",)),
        name="out_proj",
    )(o, w.astype(_BF16), x)


def memory_kv(mem, g_norm, w_kv, k_g):
    n, m, _ = mem.shape
    kv = norm_proj(mem.reshape(n * m, -1), g_norm, w_kv)[:, :w_kv.shape[1]].reshape(n, m, 2, MEM_HEADS, HEAD_DIM)
    return jnp.stack([rms_norm(kv[:, :, 0], k_g), kv[:, :, 1]], axis=2)


def memory_attend(zq, mkv, q_g):
    n, t, _ = zq.shape
    q = rms_norm(zq.reshape(n, t, MEM_HEADS, HEAD_DIM), q_g)
    s = jnp.einsum("nthd,nmhd->nhtm", q, mkv[:, :, 0]) * (HEAD_DIM ** -0.5)
    p = jax.nn.softmax(s.astype(jnp.float32), axis=-1).astype(zq.dtype)
    return jnp.einsum("nhtm,nmhd->nthd", p, mkv[:, :, 1]).reshape(n, t, MEM_W)


def nsa_split(z, q_g, k_g):
    n, t, _ = z.shape
    G, R, HD = NSA_KV_HEADS, NSA_GROUP, HEAD_DIM
    q = rms_norm(z[..., :MIX_W].reshape(n, t, G, R, HD), q_g)
    o = MIX_W
    cmp_kv = z[..., o:o + NSA_KV_W].reshape(n, t, 2, G, HD)
    sel = z[..., o + NSA_KV_W:o + 2 * NSA_KV_W].reshape(n, t, 2, G, HD)
    win = z[..., o + 2 * NSA_KV_W:o + 3 * NSA_KV_W].reshape(n, t, 2, G, HD)
    sel_kv = jnp.stack([rms_norm(sel[:, :, 0], k_g[1]), sel[:, :, 1]], axis=2)
    win_kv = jnp.stack([rms_norm(win[:, :, 0], k_g[2]), win[:, :, 1]], axis=2)
    o = o + 3 * NSA_KV_W
    gates = jax.nn.sigmoid(z[..., o:o + 3 * NSA_HEADS]).reshape(n, t, G, R, 3)
    return q, cmp_kv, sel_kv, win_kv, gates


def compress(rows, pe, w1, b1, w2):
    n, length, G, HD = rows.shape
    n_ch = length // CMP_STRIDE
    ch = rows[:, :n_ch * CMP_STRIDE].reshape(n, n_ch, CMP_STRIDE, G, HD)
    a = jnp.einsum("ncsgd,ksdh->nckgh", ch, w1.reshape(2, CMP_STRIDE, HD, CMP_HIDDEN))
    bias = pe.reshape(-1) @ w1 + b1
    pre = a[:, :-1, 0] + a[:, 1:, 1] + bias
    return jax.nn.gelu(pre, approximate=False) @ w2


def selection_map(n_cmp, n_sel):
    start = np.arange(n_cmp)[:, None] * CMP_STRIDE
    blk = np.arange(n_sel)[None, :] * SEL_BLOCK
    m = (start <= blk + SEL_BLOCK - 1) & (start + CMP_LEN - 1 >= blk)
    return jnp.asarray(m.astype(np.float32))


def nsa_block(q, q_pos, kc, vc, c_end, ks, vs, kw, vw, w_pos, gates, sel_map):
    scale = HEAD_DIM ** -0.5
    tq = q.shape[0]
    n_sel = sel_map.shape[1]
    m_c = (c_end[None, :] <= q_pos[:, None])[:, None, None, :]
    p_c = masked_softmax(jnp.einsum("tgrd,ngd->tgrn", q, kc) * scale, m_c)
    o_c = jnp.einsum("tgrn,ngd->tgrd", p_c.astype(vc.dtype), vc)
    imp = jnp.einsum("tgn,ns->tgs", jnp.sum(p_c, axis=2), sel_map)
    blk = jnp.arange(n_sel)[None, :]
    cur = (q_pos // SEL_BLOCK)[:, None]
    forced = (blk == 0) | (blk == cur) | (blk == cur - 1)
    valid = blk * SEL_BLOCK <= q_pos[:, None]
    bonus = jnp.where(forced, FORCE_BONUS, 0.0)
    imp = jnp.where(valid[:, None, :], imp + bonus[:, None, :], NEG)
    _, idx = lax.top_k(imp, min(N_SEL, n_sel))
    k_eff = idx.shape[-1]
    g_idx = jnp.arange(NSA_KV_HEADS)[None, :, None]

    def gather_blocks(a):
        a = a.reshape(n_sel, SEL_BLOCK, NSA_KV_HEADS, HEAD_DIM).transpose(2, 0, 1, 3)
        return a[g_idx, idx].reshape(tq, NSA_KV_HEADS, k_eff * SEL_BLOCK, HEAD_DIM)

    k_s = gather_blocks(ks)
    v_s = gather_blocks(vs)
    pos_s = (idx[..., None] * SEL_BLOCK + jnp.arange(SEL_BLOCK)).reshape(tq, NSA_KV_HEADS, k_eff * SEL_BLOCK)
    m_s = (pos_s <= q_pos[:, None, None])[:, :, None, :]
    p_s = masked_softmax(jnp.einsum("tgrd,tgld->tgrl", q, k_s) * scale, m_s)
    o_s = jnp.einsum("tgrl,tgld->tgrd", p_s.astype(v_s.dtype), v_s)
    dist = q_pos[:, None] - w_pos[None, :]
    m_w = ((dist >= 0) & (dist <= WINDOW) & (w_pos[None, :] >= 0))[:, None, None, :]
    p_w = masked_softmax(jnp.einsum("tgrd,lgd->tgrl", q, kw) * scale, m_w)
    o_w = jnp.einsum("tgrl,lgd->tgrd", p_w.astype(vw.dtype), vw)
    return gates[..., 0:1] * o_c + gates[..., 1:2] * o_s + gates[..., 2:3] * o_w


def nsa_prompt(z, q_g, k_g, pe, w1, b1, w2):
    n, s, _ = z.shape
    q, cmp_kv, sel_kv, win_kv, gates = nsa_split(z, q_g, k_g)
    kc = rms_norm(compress(cmp_kv[:, :, 0], pe[0], w1[0], b1[0], w2[0]), k_g[0])
    vc = compress(cmp_kv[:, :, 1], pe[1], w1[1], b1[1], w2[1])
    out = nsa_prompt_attend(q, kc, vc, sel_kv, win_kv, gates)
    pages = lambda a: a.reshape(n, s // PAGE_SIZE, PAGE_SIZE, 2, NSA_KV_HEADS, HEAD_DIM)
    wb = min(WINDOW, s)
    return out, pages(cmp_kv), pages(sel_kv), win_kv[:, s - wb:]


_NSA_SEL_TILE = 512


def _softmax_rows(s, mask):
    sm = jnp.where(mask, s, NEG)
    e = jnp.exp(sm - jnp.max(sm, axis=-1, keepdims=True))
    return jnp.where(mask, e / jnp.sum(e, axis=-1, keepdims=True), 0.0)


def _nsa_prompt_kernel(q_ref, gate_ref, kct_ref, vc_ref, smap_ref, kst_ref, vs_ref, kwt_ref, vw_ref, o_ref,
                       m_scr, l_scr, acc_scr):
    qb = pl.program_id(1)
    tq = q_ref.shape[3]
    n_cmp = kct_ref.shape[4]
    n_sel = smap_ref.shape[1]
    tk = kst_ref.shape[4]
    n_top = min(N_SEL, n_sel)
    scale = HEAD_DIM ** -0.5
    s0 = qb * tq
    pos = s0 + lax.broadcasted_iota(jnp.int32, (tq, 1), 0)
    pos3 = jnp.concatenate([pos] * NSA_GROUP, axis=0)
    gates = gate_ref[0]
    cend = lax.broadcasted_iota(jnp.int32, (1, n_cmp), 1) * CMP_STRIDE + (CMP_LEN - 1)
    blk = lax.broadcasted_iota(jnp.int32, (1, n_sel), 1)
    cur = pos // SEL_BLOCK
    forced = (blk == 0) | (blk == cur) | (blk == cur - 1)
    valid = blk * SEL_BLOCK <= pos
    n_tiles = (s0 + tq + tk - 1) // tk
    n_wt = WINDOW // tq + 1
    wt0 = jnp.maximum(qb - WINDOW // tq, 0)
    kposw = wt0 * tq + lax.broadcasted_iota(jnp.int32, (1, n_wt * tq), 1)
    dist = pos3 - kposw
    mw = (dist >= 0) & (dist <= WINDOW)

    for g in range(NSA_KV_HEADS):
        cols = slice(g * HEAD_DIM, (g + 1) * HEAD_DIM)
        qg = q_ref[0, g].reshape(NSA_GROUP * tq, HEAD_DIM)
        qg_bf = qg.astype(_BF16)
        q_lo = (qg - qg_bf.astype(_F32)).astype(_BF16)
        k_hi, k_lo = kct_ref[0, g, 0], kct_ref[0, g, 1]
        sc = (jnp.dot(qg_bf, k_hi, preferred_element_type=_F32) + jnp.dot(q_lo, k_hi, preferred_element_type=_F32)
              + jnp.dot(qg_bf, k_lo, preferred_element_type=_F32)) * scale
        p = _softmax_rows(sc, cend <= pos3)
        oc = jnp.dot(p.astype(_BF16), vc_ref[0, g], preferred_element_type=_F32)
        psum = p[0:tq]
        for r in range(1, NSA_GROUP):
            psum = psum + p[r * tq:(r + 1) * tq]
        p_hi = psum.astype(_BF16)
        rest = psum - p_hi.astype(_F32)
        p_mid = rest.astype(_BF16)
        p_lo = (rest - p_mid.astype(_F32)).astype(_BF16)
        smap = smap_ref[...]
        imp = (jnp.dot(p_hi, smap, preferred_element_type=_F32) + jnp.dot(p_mid, smap, preferred_element_type=_F32)
               + jnp.dot(p_lo, smap, preferred_element_type=_F32))
        imp = jnp.where(valid, imp + jnp.where(forced, FORCE_BONUS, 0.0), NEG)
        _, rank_t = _top_rows(imp.T, n_top)
        sel = (rank_t < float(n_top)).astype(_F32).T.astype(_BF16)
        m_scr[...] = jnp.full(m_scr.shape, NEG, _F32)
        l_scr[...] = jnp.zeros(l_scr.shape, _F32)
        acc_scr[...] = jnp.zeros(acc_scr.shape, _F32)

        def body(kt, carry):
            s = jnp.dot(qg_bf, kst_ref[0, g, kt], preferred_element_type=_F32) * scale
            brow = lax.broadcasted_iota(jnp.int32, (n_sel, tk), 0)
            bkey = kt * (tk // SEL_BLOCK) + lax.broadcasted_iota(jnp.int32, (n_sel, tk), 1) // SEL_BLOCK
            selx = jnp.dot(sel, (brow == bkey).astype(_BF16), preferred_element_type=_F32)
            kpos = kt * tk + lax.broadcasted_iota(jnp.int32, (1, tk), 1)
            mk = jnp.where((selx > 0.5) & (kpos <= pos), 1.0, 0.0)
            mk3 = jnp.concatenate([mk] * NSA_GROUP, axis=0) > 0.5
            sm = jnp.where(mk3, s, NEG)
            m_old = m_scr[...]
            m_new = jnp.maximum(m_old, jnp.max(sm, axis=-1, keepdims=True))
            alpha = jnp.exp(m_old - m_new)
            pe = jnp.where(mk3, jnp.exp(sm - m_new), 0.0)
            l_scr[...] = alpha * l_scr[...] + jnp.sum(pe, axis=-1, keepdims=True)
            acc_scr[...] = alpha * acc_scr[...] + jnp.dot(pe.astype(_BF16), vs_ref[0, kt],
                                                          preferred_element_type=_F32)
            m_scr[...] = m_new
            return carry

        lax.fori_loop(0, n_tiles, body, 0)
        osel = acc_scr[:, cols] / l_scr[...]
        kwt = jnp.concatenate([kwt_ref[0, g, wt0 + i] for i in range(n_wt)], axis=1)
        vw = jnp.concatenate([vw_ref[0, wt0 + i] for i in range(n_wt)], axis=0)
        sw = jnp.dot(qg_bf, kwt, preferred_element_type=_F32) * scale
        pw = _softmax_rows(sw, mw)
        ow = jnp.dot(pw.astype(_BF16), vw, preferred_element_type=_F32)[:, cols]
        for r in range(NSA_GROUP):
            c = (g * NSA_GROUP + r) * 3
            rows = slice(r * tq, (r + 1) * tq)
            o_ref[0, g, r] = (gates[:, c:c + 1] * oc[rows] + gates[:, c + 1:c + 2] * osel[rows]
                              + gates[:, c + 2:c + 3] * ow[rows])


def nsa_prompt_attend(q, kc, vc, sel_kv, win_kv, gates):
    n, s = q.shape[:2]
    G, R, HD = NSA_KV_HEADS, NSA_GROUP, HEAD_DIM
    tq, tk = NSA_QBLOCK, _NSA_SEL_TILE
    n_cmp = kc.shape[1]
    ncp = s // CMP_STRIDE
    n_sel = s // SEL_BLOCK
    assert s % tk == 0 and s >= WINDOW + tq
    padc = ((0, 0), (0, ncp - n_cmp), (0, 0), (0, 0))
    qh = q.transpose(0, 2, 3, 1, 4)
    kct = jnp.pad(kc, padc).transpose(0, 2, 3, 1)
    kct_hi = kct.astype(_BF16)
    kct = jnp.stack([kct_hi, (kct - kct_hi.astype(_F32)).astype(_BF16)], axis=2)
    vcb = jnp.pad(vc, padc).transpose(0, 2, 1, 3).astype(_BF16)
    smap = jnp.pad(selection_map(n_cmp, n_sel), ((0, ncp - n_cmp), (0, 0))).astype(_BF16)
    kst = sel_kv[:, :, 0].astype(_BF16).reshape(n, s // tk, tk, G, HD).transpose(0, 3, 1, 4, 2)
    vs = sel_kv[:, :, 1].astype(_BF16).reshape(n, s // tk, tk, G * HD)
    kwt = win_kv[:, :, 0].astype(_BF16).reshape(n, s // tq, tq, G, HD).transpose(0, 3, 1, 4, 2)
    vw = win_kv[:, :, 1].astype(_BF16).reshape(n, s // tq, tq, G * HD)
    once = pl.Buffered(1)
    out = pl.pallas_call(
        _nsa_prompt_kernel,
        grid=(n, s // tq),
        in_specs=[
            pl.BlockSpec((1, G, R, tq, HD), lambda b, i: (b, 0, 0, i, 0)),
            pl.BlockSpec((1, tq, G * R * 3), lambda b, i: (b, i, 0)),
            pl.BlockSpec((1, G, 2, HD, ncp), lambda b, i: (b, 0, 0, 0, 0)),
            pl.BlockSpec((1, G, ncp, HD), lambda b, i: (b, 0, 0, 0)),
            pl.BlockSpec((ncp, n_sel), lambda b, i: (0, 0)),
            pl.BlockSpec((1, G, s // tk, HD, tk), lambda b, i: (b, 0, 0, 0, 0), pipeline_mode=once),
            pl.BlockSpec((1, s // tk, tk, G * HD), lambda b, i: (b, 0, 0, 0), pipeline_mode=once),
            pl.BlockSpec((1, G, s // tq, HD, tq), lambda b, i: (b, 0, 0, 0, 0), pipeline_mode=once),
            pl.BlockSpec((1, s // tq, tq, G * HD), lambda b, i: (b, 0, 0, 0), pipeline_mode=once),
        ],
        out_specs=pl.BlockSpec((1, G, R, tq, HD), lambda b, i: (b, 0, 0, i, 0)),
        out_shape=jax.ShapeDtypeStruct((n, G, R, s, HD), _F32),
        scratch_shapes=[pltpu.VMEM((R * tq, 1), _F32), pltpu.VMEM((R * tq, 1), _F32),
                        pltpu.VMEM((R * tq, G * HD), _F32)],
        compiler_params=pltpu.CompilerParams(dimension_semantics=("arbitrary", "arbitrary"),
                                             vmem_limit_bytes=VMEM_BYTES_V7X * 3 // 4),
        name="nsa_prompt",
    )(qh, gates.reshape(n, s, G * R * 3), kct, vcb, smap, kst, vs, kwt, vw)
    return out.transpose(0, 3, 1, 2, 4).reshape(n, s, MIX_W)


_BF16_ROWS = 16


def _softmax_cols_aug(sT, mask, v_aug):
    sm = jnp.where(mask, sT, NEG)
    e = jnp.where(mask, jnp.exp(sm - jnp.max(sm, axis=0, keepdims=True)), 0.0)
    acc = lax.dot_general(e.astype(_BF16), v_aug, (((0,), (0,)), ((), ())), preferred_element_type=_F32)
    c = v_aug.shape[1] - LANES
    return acc[:, :c] / acc[:, c:c + 1]


def _nsa_sample_kernel(pt_ref, qbd_ref, kc_ref, vc_ref, smapt_ref, rsum_ref, rexp_ref, gate_ref, *rest,
                       n_pages, past, t_new, n_cmp, n_sel, wb):
    page_refs = rest[:n_pages]
    selnew_ref, win_ref, o_ref, s_scr = rest[n_pages:]
    kw = NSA_KV_HEADS * HEAD_DIM
    scale = HEAD_DIM ** -0.5
    n_top = min(N_SEL, n_sel)
    qbd = qbd_ref[0]
    qbd_bf = qbd.astype(_BF16)
    nl = qbd.shape[1]
    pos = past + lax.broadcasted_iota(jnp.int32, (1, nl), 1) % t_new
    ones_cols = lambda k: jnp.ones((k, LANES), _BF16)

    ncp = kc_ref.shape[1]
    ci = lax.broadcasted_iota(jnp.int32, (ncp, 1), 0)
    mc = (ci * CMP_STRIDE + (CMP_LEN - 1) <= pos) & (ci < n_cmp)
    sc = jnp.dot(kc_ref[0], qbd, precision=_HI, preferred_element_type=_F32) * scale
    scm = jnp.where(mc, sc, NEG)
    e = jnp.exp(scm - jnp.max(scm, axis=0, keepdims=True))
    pc = jnp.where(mc, e / jnp.sum(e, axis=0, keepdims=True), 0.0)
    oc = lax.dot_general(pc.astype(_BF16), vc_ref[0], (((0,), (0,)), ((), ())), preferred_element_type=_F32)
    psum = jnp.dot(pc, rsum_ref[...], precision=_HI, preferred_element_type=_F32)
    imp = jnp.dot(smapt_ref[...], psum, precision=_HI, preferred_element_type=_F32)
    nb, nq = imp.shape
    pos_q = past + lax.broadcasted_iota(jnp.int32, (1, nq), 1) % t_new
    blk = lax.broadcasted_iota(jnp.int32, (nb, 1), 0)
    cur = pos_q // SEL_BLOCK
    forced = (blk == 0) | (blk == cur) | (blk == cur - 1)
    imp = jnp.where(blk * SEL_BLOCK <= pos_q, imp + jnp.where(forced, FORCE_BONUS, 0.0), NEG)
    imp = jnp.where(blk < n_sel, imp, -jnp.inf)
    _, rank = _top_rows(imp, n_top)
    sel = jnp.dot((rank < float(n_top)).astype(_BF16), rexp_ref[...], preferred_element_type=_F32)

    rowi = lax.broadcasted_iota(jnp.int32, (PAGE_SIZE, 1), 0)
    per_page = PAGE_SIZE // SEL_BLOCK
    for p in range(n_pages):
        kp = page_refs[p][0, :, 0:kw].astype(_BF16)
        s = jnp.dot(kp, qbd_bf, preferred_element_type=_F32) * scale
        pick = sel[p * per_page:p * per_page + 1, :]
        for b in range(1, per_page):
            pick = jnp.where(rowi >= b * SEL_BLOCK, sel[p * per_page + b:p * per_page + b + 1, :], pick)
        ok = (pick > 0.5) & (p * PAGE_SIZE + rowi <= pos)
        s_scr[p * PAGE_SIZE:(p + 1) * PAGE_SIZE, :] = jnp.where(ok, s, NEG)
    n_new = selnew_ref.shape[1]
    rown = lax.broadcasted_iota(jnp.int32, (n_new, 1), 0)
    kn = selnew_ref[0, :, 0:kw].astype(_BF16).astype(_F32)
    s = jnp.dot(kn, qbd_bf.astype(_F32), precision=_HI, preferred_element_type=_F32) * scale
    ok = (sel[past // SEL_BLOCK:past // SEL_BLOCK + 1, :] > 0.5) & (past + rown <= pos)
    s_scr[past:past + n_new, :] = jnp.where(ok, s, NEG)
    mx = jnp.max(s_scr[...], axis=0, keepdims=True)
    acc = jnp.zeros((nl, kw + LANES), _F32)
    for p in range(n_pages):
        sp = s_scr[p * PAGE_SIZE:(p + 1) * PAGE_SIZE, :]
        ep = jnp.where(sp > 0.5 * NEG, jnp.exp(sp - mx), 0.0).astype(_BF16)
        v_aug = jnp.concatenate([page_refs[p][0, :, kw:2 * kw].astype(_BF16), ones_cols(PAGE_SIZE)], axis=1)
        acc = acc + lax.dot_general(ep, v_aug, (((0,), (0,)), ((), ())), preferred_element_type=_F32)
    sp = s_scr[past:past + n_new, :]
    ep = jnp.where(sp > 0.5 * NEG, jnp.exp(sp - mx), 0.0).astype(_BF16)
    v_aug = jnp.concatenate([selnew_ref[0, :, kw:2 * kw].astype(_BF16), ones_cols(n_new)], axis=1)
    acc = acc + lax.dot_general(ep, v_aug, (((0,), (0,)), ((), ())), preferred_element_type=_F32)
    osel = acc[:, :kw] / acc[:, kw:kw + 1]

    n_w = win_ref.shape[1]
    wpos = past - wb + lax.broadcasted_iota(jnp.int32, (n_w, 1), 0)
    dist = pos - wpos
    mw = (dist >= 0) & (dist <= WINDOW) & (wpos >= 0)
    sw = jnp.dot(win_ref[0, :, 0:kw].astype(_BF16), qbd_bf, preferred_element_type=_F32) * scale
    v_aug = jnp.concatenate([win_ref[0, :, kw:2 * kw].astype(_BF16), ones_cols(n_w)], axis=1)
    ow = _softmax_cols_aug(sw, mw, v_aug)

    g = gate_ref[0]
    o_ref[0] = g[:, 0:1] * oc + g[:, 1:2] * osel + g[:, 2:3] * ow


def nsa_sample_attend(q, kc, vc, pool_sel, page_table, sel_new, win_all, gates, wb):
    n, t_new = q.shape[:2]
    G, R, HD = NSA_KV_HEADS, NSA_GROUP, HEAD_DIM
    n_pages = page_table.shape[1]
    past = n_pages * PAGE_SIZE
    n_cmp = kc.shape[1]
    n_sel = -(-(past + t_new) // SEL_BLOCK)
    assert past % SEL_BLOCK == 0 and t_new <= SEL_BLOCK and PAGE_SIZE % SEL_BLOCK == 0
    kw = G * HD
    nl = G * R * t_new
    up = lambda v, m: -(-v // m) * m
    ncp, nbp = up(n_cmp, LANES), up(n_sel, 8)
    n_new, n_w = up(t_new, _BF16_ROWS), up(wb + t_new, _BF16_ROWS)
    qbd = jnp.einsum("ntgrd,gh->ngdhrt", q, jnp.eye(G, dtype=q.dtype)).reshape(n, kw, nl)
    kcp = jnp.pad(kc.reshape(n, n_cmp, kw), ((0, 0), (0, ncp - n_cmp), (0, 0)))
    vcp = jnp.pad(vc.reshape(n, n_cmp, kw), ((0, 0), (0, ncp - n_cmp), (0, 0))).astype(_BF16)
    smapt = jnp.pad(selection_map(n_cmp, n_sel).T, ((0, nbp - n_sel), (0, ncp - n_cmp)))
    rsum = np.zeros((G, R, t_new, G, t_new), np.float32)
    for g in range(G):
        for t in range(t_new):
            rsum[g, :, t, g, t] = 1.0
    rsum = rsum.reshape(nl, G * t_new)
    gates3 = gates.transpose(0, 2, 3, 1, 4).reshape(n, nl, 3)
    pages = pool_sel.reshape(pool_sel.shape[0], PAGE_SIZE, 2 * kw)
    seln = jnp.pad(sel_new.reshape(n, t_new, 2 * kw), ((0, 0), (0, n_new - t_new), (0, 0)))
    win = jnp.pad(win_all.reshape(n, wb + t_new, 2 * kw), ((0, 0), (0, n_w - wb - t_new), (0, 0)))
    whole = lambda *shape: pl.BlockSpec(shape, lambda b, pt: (0,) * len(shape))
    per_seq = lambda *shape: pl.BlockSpec((1,) + shape, lambda b, pt: (b,) + (0,) * len(shape))
    page_spec = lambda p: pl.BlockSpec((1, PAGE_SIZE, 2 * kw), lambda b, pt: (pt[b, p], 0, 0))
    out = pl.pallas_call(
        functools.partial(_nsa_sample_kernel, n_pages=n_pages, past=past, t_new=t_new, n_cmp=n_cmp, n_sel=n_sel,
                          wb=wb),
        grid_spec=pltpu.PrefetchScalarGridSpec(
            num_scalar_prefetch=1,
            grid=(n,),
            in_specs=[per_seq(kw, nl), per_seq(ncp, kw), per_seq(ncp, kw), whole(nbp, ncp),
                      whole(nl, G * t_new), whole(G * t_new, nl), per_seq(nl, 3)]
                     + [page_spec(p) for p in range(n_pages)]
                     + [per_seq(n_new, 2 * kw), per_seq(n_w, 2 * kw)],
            out_specs=per_seq(nl, kw),
            scratch_shapes=[pltpu.VMEM((past + n_new, nl), _F32)],
        ),
        out_shape=jax.ShapeDtypeStruct((n, nl, kw), _F32),
        compiler_params=pltpu.CompilerParams(dimension_semantics=("arbitrary",),
                                             vmem_limit_bytes=VMEM_BYTES_V7X // 2),
        name="nsa_sample",
    )(page_table, qbd, kcp, vcp, smapt, jnp.asarray(rsum), jnp.asarray(rsum.T).astype(_BF16), gates3,
      *([pages] * n_pages), seln, win)
    o = out.reshape(n, G, R, t_new, G, HD)
    o = jnp.stack([o[:, g, :, :, g, :] for g in range(G)], axis=1)
    return o.transpose(0, 3, 1, 2, 4).reshape(n, t_new, MIX_W)


def nsa_sample(z, pool_cmp, pool_sel, win_buf, page_table, q_g, k_g, pe, w1, b1, w2):
    n, t_new, _ = z.shape
    past = page_table.shape[1] * PAGE_SIZE
    G, HD = NSA_KV_HEADS, HEAD_DIM
    q, cmp_new, sel_new, win_new, gates = nsa_split(z, q_g, k_g)
    win_all = jnp.concatenate([win_buf, win_new], axis=1)
    wb = win_buf.shape[1]
    n_ch = (past + t_new) // CMP_STRIDE
    assert PAGE_SIZE % CMP_STRIDE == 0 and n_ch * CMP_STRIDE <= past
    per_page = PAGE_SIZE // CMP_STRIDE
    pool6 = pool_cmp.reshape(pool_cmp.shape[0], per_page, CMP_STRIDE, 2, G, HD)
    w1r = w1.reshape(2, 2, CMP_STRIDE, HD, CMP_HIDDEN)
    a_pool = jnp.einsum("pcskgd,kjsdh->pckjgh", pool6, w1r)
    a_seq = a_pool[page_table].reshape(n, past // CMP_STRIDE, 2, 2, G, CMP_HIDDEN)[:, :n_ch]
    bias = jnp.einsum("kx,kxh->kh", pe.reshape(2, -1), w1) + b1
    pre = a_seq[:, :-1, :, 0] + a_seq[:, 1:, :, 1] + bias[None, None, :, None, :]
    comp = jnp.einsum("nckgh,khd->nckgd", jax.nn.gelu(pre, approximate=False), w2)
    kc = rms_norm(comp[:, :, 0], k_g[0])
    vc = comp[:, :, 1]
    out = nsa_sample_attend(q, kc, vc, pool_sel, page_table, sel_new, win_all, gates, wb)
    return out, cmp_new, sel_new, win_all[:, -wb:]


def shortconv_mix(z, ctx, w, b):
    bg = z[..., :MIX_W]
    cg = z[..., MIX_W:2 * MIX_W]
    v = z[..., 2 * MIX_W:3 * MIX_W]
    u_ext = jnp.concatenate([ctx, cg * v], axis=1)
    y = bg * causal_dwconv(u_ext, w, b)
    return y, u_ext[:, -(CONV_B_WIDTH - 1):]


def conformer_mix(z, ctx, w, b, ln_g, ln_b):
    u = z[..., :MIX_W] * jax.nn.sigmoid(z[..., MIX_W:2 * MIX_W])
    u_ext = jnp.concatenate([ctx, u], axis=1)
    y = jax.nn.silu(layer_norm(causal_dwconv(u_ext, w, b), ln_g, ln_b))
    return y, u_ext[:, -(CONV_C_WIDTH - 1):]


_CAND = [(j1, j2) for j1 in range(PEER_TOPK) for j2 in range(PEER_TOPK) if (j1 + 1) * (j2 + 1) <= PEER_TOPK]
_N_CAND = len(_CAND)
_CAND_ROWS = -(-_N_CAND // 8) * 8
_PEER_SEL_TILE = 256
_PEER_TOK_TILE = 512
_PEER_EXP_BLOCK = 1024
_PEER_SUB_BLOCK = 256
_PEER_ROW_GROUP = 32


def _cand_tables():
    j1 = np.full((_CAND_ROWS, LANES), -1.0, np.float32)
    j2 = np.full((_CAND_ROWS, LANES), -1.0, np.float32)
    flat = np.full((_CAND_ROWS, LANES), 1.0e6, np.float32)
    for r, (a, b) in enumerate(_CAND):
        j1[r], j2[r], flat[r] = a, b, a * PEER_TOPK + b
    return np.stack([j1, j2, flat])


def _gelu_erf(x):
    return 0.5 * x * (1.0 + lax.erf(x * (2.0 ** -0.5)))


def _top_rows(s, n_top):
    (vals, rank), = _top_rows_many([s], n_top)
    return vals, rank


def _top_rows_many(arrays, n_top):
    k, l = arrays[0].shape
    kidx = lax.broadcasted_iota(jnp.int32, (k, l), 0).astype(_F32)
    jrow = lax.broadcasted_iota(jnp.int32, (n_top, l), 0)
    work = list(arrays)
    rank = [jnp.full((k, l), float(n_top), _F32) for _ in arrays]
    vals = [jnp.zeros((n_top, l), _F32) for _ in arrays]
    for j in range(n_top):
        for i in range(len(arrays)):
            m = jnp.max(work[i], axis=0, keepdims=True)
            first = jnp.min(jnp.where(work[i] == m, kidx, float(k)), axis=0, keepdims=True)
            sel = kidx == first
            rank[i] = jnp.where(sel, float(j), rank[i])
            work[i] = jnp.where(sel, -jnp.inf, work[i])
            vals[i] = jnp.where(jrow == j, m, vals[i])
    return list(zip(vals, rank))


def _peer_select_kernel(x_ref, g_ref, wq_hi_ref, wq_lo_ref, sk_ref, ct_ref, ht_ref, p1_ref, c1_ref, p2_ref, r2_ref,
                        h_hi_scr, h_lo_scr):
    hd = pl.program_id(1)

    @pl.when(hd == 0)
    def _():
        x = x_ref[...]
        h = x * lax.rsqrt(jnp.mean(x * x, axis=-1, keepdims=True) + EPS) * g_ref[...]
        h_hi = h.astype(_BF16)
        h_hi_scr[...] = h_hi
        h_lo_scr[...] = (h - h_hi.astype(_F32)).astype(_BF16)
        ht_ref[...] = h.T.astype(_BF16)

    h_hi, h_lo = h_hi_scr[...], h_lo_scr[...]
    q = (jnp.dot(h_hi, wq_hi_ref[...], preferred_element_type=_F32)
         + jnp.dot(h_lo, wq_hi_ref[...], preferred_element_type=_F32)
         + jnp.dot(h_hi, wq_lo_ref[...], preferred_element_type=_F32))
    nt = (((1,), (1,)), ((), ()))
    half = PEER_DK // 2
    q_hi = q.astype(_BF16)
    q_lo = (q - q_hi.astype(_F32)).astype(_BF16)

    def scores(c, cols):
        k_hi, k_lo = sk_ref[0, c, 0], sk_ref[0, c, 1]
        return (lax.dot_general(k_hi, q_hi[:, cols], nt, preferred_element_type=_F32)
                + lax.dot_general(k_lo, q_hi[:, cols], nt, preferred_element_type=_F32)
                + lax.dot_general(k_hi, q_lo[:, cols], nt, preferred_element_type=_F32))

    s1_all = scores(0, slice(0, half))
    s2_all = scores(1, slice(half, PEER_DK))
    cj1, cj2, cflat = ct_ref[0], ct_ref[1], ct_ref[2]
    for c in range(x_ref.shape[0] // LANES):
        sl = slice(c * LANES, (c + 1) * LANES)
        s1, s2 = s1_all[:, sl], s2_all[:, sl]
        (v1, rank1), (v2, rank2) = _top_rows_many([s1, s2], PEER_TOPK)
        a1 = jnp.zeros((_CAND_ROWS, LANES), _F32)
        a2 = jnp.zeros((_CAND_ROWS, LANES), _F32)
        for j in range(PEER_TOPK):
            a1 = jnp.where(cj1 == float(j), v1[j:j + 1, :], a1)
            a2 = jnp.where(cj2 == float(j), v2[j:j + 1, :], a2)
        cand = jnp.where(cj1 >= 0.0, a1 + a2, -jnp.inf)
        work = cand
        picked = jnp.zeros((_CAND_ROWS, LANES), _F32)
        for _ in range(PEER_TOPK):
            m = jnp.max(work, axis=0, keepdims=True)
            first = jnp.min(jnp.where(work == m, cflat, 2.0e6), axis=0, keepdims=True)
            sel = cflat == first
            picked = jnp.where(sel, 1.0, picked)
            work = jnp.where(sel, -jnp.inf, work)
        top = v1[0:1, :] + v2[0:1, :]
        z = jnp.sum(jnp.where(picked > 0.0, jnp.exp(cand - top), 0.0), axis=0, keepdims=True)
        c1 = jnp.zeros((PEER_KEYS, LANES), _F32)
        for j in range(PEER_TOPK):
            cnt_j = jnp.sum(jnp.where(cj1 == float(j), picked, 0.0), axis=0, keepdims=True)
            c1 = jnp.where(rank1 == float(j), cnt_j, c1)
        p1_ref[0, :, sl] = jnp.exp(s1 - v1[0:1, :]) / z
        c1_ref[0, :, sl] = c1
        p2_ref[0, :, sl] = jnp.exp(s2 - v2[0:1, :])
        r2_ref[0, :, sl] = rank2


def _peer_select(x, g, w_q, sub_keys):
    n, d = x.shape
    ts = _PEER_SEL_TILE
    assert n % ts == 0
    w_hi = w_q.astype(_BF16)
    w_lo = (w_q - w_hi.astype(_F32)).astype(_BF16)
    sk_hi = sub_keys.astype(_BF16)
    sk = jnp.stack([sk_hi, (sub_keys - sk_hi.astype(_F32)).astype(_BF16)], axis=2)
    fac = jax.ShapeDtypeStruct((PEER_HEADS, PEER_KEYS, n), _F32)
    fac_spec = pl.BlockSpec((1, PEER_KEYS, ts), lambda i, h: (h, 0, i))
    return pl.pallas_call(
        _peer_select_kernel,
        grid=(n // ts, PEER_HEADS),
        in_specs=[
            pl.BlockSpec((ts, d), lambda i, h: (i, 0)),
            pl.BlockSpec((1, d), lambda i, h: (0, 0)),
            pl.BlockSpec((d, PEER_DK), lambda i, h: (0, h)),
            pl.BlockSpec((d, PEER_DK), lambda i, h: (0, h)),
            pl.BlockSpec((1, 2, 2, PEER_KEYS, PEER_DK // 2), lambda i, h: (h, 0, 0, 0, 0)),
            pl.BlockSpec((3, _CAND_ROWS, LANES), lambda i, h: (0, 0, 0)),
        ],
        out_specs=[pl.BlockSpec((d, ts), lambda i, h: (0, i)), fac_spec, fac_spec, fac_spec, fac_spec],
        out_shape=[jax.ShapeDtypeStruct((d, n), _BF16), fac, fac, fac, fac],
        scratch_shapes=[pltpu.VMEM((ts, d), _BF16), pltpu.VMEM((ts, d), _BF16)],
        compiler_params=pltpu.CompilerParams(dimension_semantics=("arbitrary", "arbitrary"),
                                             vmem_limit_bytes=VMEM_BYTES_V7X * 3 // 4),
        name="peer_select",
    )(x, g.reshape(1, d), w_hi, w_lo, sk, jnp.asarray(_cand_tables()))


def _peer_dense_kernel(x_ref, ht_ref, u_ref, vt_ref, p1_ref, c1_ref, p2_ref, r2_ref, o_ref,
                       acc_ref, *scratch):
    j = pl.program_id(1)

    @pl.when(j == 0)
    def _():
        acc_ref[...] = jnp.zeros_like(acc_ref)

    eb, tt = u_ref.shape[0], ht_ref.shape[1]
    sub = _PEER_SUB_BLOCK
    n_sub = eb // sub
    st_refs, wact_refs = scratch[:2], scratch[2:]
    half = tt // 2
    d = vt_ref.shape[0]
    n_out = d // sub

    def scores(k, nh):
        lanes = slice(nh * half, (nh + 1) * half)
        st_refs[k % 2][:, lanes] = jnp.dot(u_ref[k * sub:(k + 1) * sub, :], ht_ref[:, lanes],
                                           preferred_element_type=_F32)

    def output_piece(k, q):
        r = slice(q * sub, (q + 1) * sub)
        acc_ref[r, :] += jnp.dot(vt_ref[r, k * sub:(k + 1) * sub], wact_refs[k % 2][...],
                                 preferred_element_type=_F32)

    def build_chunk(k, al, c, a):
        lanes = slice(c * LANES, (c + 1) * LANES)
        c1 = [c1_ref[h, pl.ds(a, 1), :][:, lanes] for h in range(PEER_HEADS)]
        p1 = [p1_ref[h, pl.ds(a, 1), :][:, lanes] for h in range(PEER_HEADS)]
        for rg in range(PEER_KEYS // _PEER_ROW_GROUP):
            b = slice(rg * _PEER_ROW_GROUP, (rg + 1) * _PEER_ROW_GROUP)
            brows = slice(al * PEER_KEYS + b.start, al * PEER_KEYS + b.stop)
            w = jnp.zeros((_PEER_ROW_GROUP, LANES), _F32)
            for h in range(PEER_HEADS):
                w = w + jnp.where(r2_ref[h, b, lanes] < c1[h], p2_ref[h, b, lanes], 0.0) * p1[h]
            wact_refs[k % 2][brows, lanes] = (w * _gelu_erf(st_refs[k % 2][brows, lanes])).astype(_BF16)

    n_al, n_c = sub // PEER_KEYS, tt // LANES
    n_chunks = n_al * n_c
    scores(0, 0)
    scores(0, 1)
    for k in range(n_sub):
        i = 0
        for al in range(n_al):
            a = j * (eb // PEER_KEYS) + k * n_al + al
            for c in range(n_c):
                build_chunk(k, al, c, a)
                if k + 1 < n_sub and i % (n_chunks // 2) == 0:
                    scores(k + 1, i // (n_chunks // 2))
                if k >= 1 and i % (n_chunks // n_out) == n_chunks // n_out - 1:
                    output_piece(k - 1, i // (n_chunks // n_out))
                i += 1
    for q in range(n_out):
        output_piece(n_sub - 1, q)

    @pl.when(j == pl.num_programs(1) - 1)
    def _():
        o_ref[...] = x_ref[...] + acc_ref[...].T


def _peer_dense(x, ht, u_bf, vt_bf, p1, c1, p2, r2):
    n, d = x.shape
    tt, eb = _PEER_TOK_TILE, _PEER_EXP_BLOCK
    assert n % tt == 0 and N_EXPERTS % eb == 0
    fac_spec = pl.BlockSpec((PEER_HEADS, PEER_KEYS, tt), lambda i, j: (0, 0, i))
    return pl.pallas_call(
        _peer_dense_kernel,
        grid=(n // tt, N_EXPERTS // eb),
        in_specs=[
            pl.BlockSpec((tt, d), lambda i, j: (i, 0)),
            pl.BlockSpec((d, tt), lambda i, j: (0, i)),
            pl.BlockSpec((eb, d), lambda i, j: (j, 0)),
            pl.BlockSpec((d, eb), lambda i, j: (0, j)),
            fac_spec, fac_spec, fac_spec, fac_spec,
        ],
        out_specs=pl.BlockSpec((tt, d), lambda i, j: (i, 0)),
        out_shape=jax.ShapeDtypeStruct((n, d), _F32),
        scratch_shapes=([pltpu.VMEM((d, tt), _F32)]
                        + [pltpu.VMEM((_PEER_SUB_BLOCK, tt), _F32)] * 2
                        + [pltpu.VMEM((_PEER_SUB_BLOCK, tt), _BF16)] * 2),
        compiler_params=pltpu.CompilerParams(dimension_semantics=("arbitrary", "arbitrary"),
                                             vmem_limit_bytes=VMEM_BYTES_V7X * 3 // 4),
        name="peer_dense",
    )(x, ht, u_bf, vt_bf, p1, c1, p2, r2)


def peer_residual(x, g, w_q, sub_keys, u_tab, v_tab):
    ht, p1, c1, p2, r2 = _peer_select(x, g, w_q, sub_keys)
    return _peer_dense(x, ht, u_tab.astype(_BF16), v_tab.T.astype(_BF16), p1, c1, p2, r2)


def kernel(x_prompt, x_sample, cache_a_cmp_kv, cache_a_sel_kv, cache_a_win_kv, state_b_conv, state_c_conv,
           cache_mem_kv, page_table, mem_prompt, norm_mix_g, norm_mem_g, w_mem_kv, mem_q_norm_g, mem_k_norm_g,
           w_out, norm_ffn_g, peer_w_q, peer_sub_keys, peer_u, peer_v, a_w_in, a_q_norm_g, a_k_norm_g,
           a_cmp_pe, a_cmp_w1, a_cmp_b1, a_cmp_w2, b_w_in, b_conv_w, b_conv_b, c_w_in, c_conv_w, c_conv_b,
           c_ln_g, c_ln_b):
    (n_p, s_len), (n_s, t_new) = x_prompt.shape[:2], x_sample.shape[:2]
    n_tok_p = n_p * s_len
    x_all = jnp.concatenate([x_prompt.reshape(-1, D_MODEL), x_sample.reshape(-1, D_MODEL)], axis=0)
    p_cmp, p_sel, p_win, p_cb, p_cc, p_mem = [], [], [], [], [], []
    s_cmp, s_sel, s_win, s_cb, s_cc = [], [], [], [], []
    for i in range(DEPTH):
        kind, li = i % N_MIXERS, i // N_MIXERS
        w_in = (a_w_in, b_w_in, c_w_in)[kind][li]
        z_all = norm_proj(x_all, norm_mix_g[i], w_in)
        zp = z_all[:n_tok_p].reshape(n_p, s_len, -1)
        zs = z_all[n_tok_p:].reshape(n_s, t_new, -1)
        mem_cols = slice(w_in.shape[1] - MEM_W, w_in.shape[1])
        if kind == 0:
            nsa_w = (a_q_norm_g[li], a_k_norm_g[li], a_cmp_pe[li], a_cmp_w1[li], a_cmp_b1[li], a_cmp_w2[li])
            mp, c_p, sl_p, w_p = nsa_prompt(zp, *nsa_w)
            ms, c_s, sl_s, w_s = nsa_sample(zs, cache_a_cmp_kv[li], cache_a_sel_kv[li], cache_a_win_kv[li],
                                            page_table, *nsa_w)
            p_cmp.append(c_p)
            p_sel.append(sl_p)
            p_win.append(w_p)
            s_cmp.append(c_s)
            s_sel.append(sl_s)
            s_win.append(w_s)
        elif kind == 1:
            mp, st_p = shortconv_mix(zp, jnp.zeros((n_p, CONV_B_WIDTH - 1, MIX_W), zp.dtype), b_conv_w[li], b_conv_b[li])
            ms, st_s = shortconv_mix(zs, state_b_conv[li], b_conv_w[li], b_conv_b[li])
            p_cb.append(st_p)
            s_cb.append(st_s)
        else:
            mp, st_p = conformer_mix(zp, jnp.zeros((n_p, CONV_C_WIDTH - 1, MIX_W), zp.dtype),
                                     c_conv_w[li], c_conv_b[li], c_ln_g[li], c_ln_b[li])
            ms, st_s = conformer_mix(zs, state_c_conv[li], c_conv_w[li], c_conv_b[li], c_ln_g[li], c_ln_b[li])
            p_cc.append(st_p)
            s_cc.append(st_s)
        mkv_p = memory_kv(mem_prompt, norm_mem_g[i], w_mem_kv[i], mem_k_norm_g[i])
        p_mem.append(mkv_p)
        op = jnp.concatenate([mp, memory_attend(zp[..., mem_cols], mkv_p, mem_q_norm_g[i])], axis=-1)
        os_ = jnp.concatenate([ms, memory_attend(zs[..., mem_cols], cache_mem_kv[i], mem_q_norm_g[i])], axis=-1)
        o_all = jnp.concatenate([op.reshape(-1, D_MODEL), os_.reshape(-1, D_MODEL)], axis=0)
        x_all = out_proj_residual(o_all, w_out[i], x_all)
        x_all = peer_residual(x_all, norm_ffn_g[i], peer_w_q[i], peer_sub_keys[i], peer_u[i], peer_v[i])
    xp = x_all[:n_tok_p].reshape(x_prompt.shape)
    xs = x_all[n_tok_p:].reshape(x_sample.shape)
    return (xp, xs, jnp.stack(p_cmp), jnp.stack(p_sel), jnp.stack(p_win), jnp.stack(p_cb), jnp.stack(p_cc),
            jnp.stack(p_mem), jnp.stack(s_cmp), jnp.stack(s_sel), jnp.stack(s_win), jnp.stack(s_cb), jnp.stack(s_cc))
```
